```python
import jax, jax.numpy as jnp
from jax import lax
import numpy as np

D_MODEL = 1024
BATCH = 8
SEQ = 8192
DEPTH = 2

GRID_W = 64
CTX_LEN = 256
N_HEADS = 8
N_KV_HEADS = 2
HEAD_DIM = 64
Q_GROUP = N_HEADS // N_KV_HEADS
ATTN_W = N_HEADS * HEAD_DIM
KV_W = N_KV_HEADS * HEAD_DIM
ROPE_AXIS_DIM = HEAD_DIM // 2
ROPE_BASE = 10000.0
Q_BLOCK = 128
CONV_W = D_MODEL - ATTN_W
MIX_IN_W = ATTN_W + 2 * KV_W + 3 * CONV_W
MIX_OUT_IN = ATTN_W + CONV_W
POOL_WINDOWS = (2, 4, 8, 16)
N_POOL_GROUPS = len(POOL_WINDOWS)
POOL_GROUP_W = D_MODEL // N_POOL_GROUPS
D_FF = 2816
N_EXPERTS = 8
TOP_K = 2
D_FF_EXPERT = 1408
N_MOD = 6
N_EVEN = (DEPTH + 1) // 2
N_ODD = DEPTH // 2
EPS = 1e-6

kernel_name = "hybrid_dit_attn_conv_pool_moe"


def rms_norm(x, g):
    xf = x.astype(jnp.float32)
    y = xf * lax.rsqrt(jnp.mean(xf * xf, axis=-1, keepdims=True) + EPS)
    return (y * g.astype(jnp.float32)).astype(x.dtype)


def modulate(h, shift, scale):
    return h * (1 + scale) + shift


def adaln(cvec, w, b):
    return jnp.split(jax.nn.silu(cvec) @ w + b, N_MOD, axis=-1)


def axial_rope_tables(n_tokens):
    rows = n_tokens // GRID_W
    row = jnp.broadcast_to(jnp.arange(rows)[:, None], (rows, GRID_W)).reshape(-1)
    col = jnp.broadcast_to(jnp.arange(GRID_W)[None, :], (rows, GRID_W)).reshape(-1)
    freqs = ROPE_BASE ** (-jnp.arange(0, ROPE_AXIS_DIM, 2, dtype=jnp.float32) / ROPE_AXIS_DIM)
    ang = jnp.stack([row.astype(jnp.float32)[:, None] * freqs,
                     col.astype(jnp.float32)[:, None] * freqs], axis=1)
    return jnp.cos(ang), jnp.sin(ang)


def apply_axial_rope(x, cos, sin):
    xr = x.astype(jnp.float32).reshape(*x.shape[:-1], 2, 2, ROPE_AXIS_DIM // 2)
    x1, x2 = xr[..., 0, :], xr[..., 1, :]
    c = cos[None, :, None]
    s = sin[None, :, None]
    out = jnp.stack([x1 * c - x2 * s, x1 * s + x2 * c], axis=-2)
    return out.reshape(x.shape).astype(x.dtype)


def attend(qg, k, v):
    s = jnp.einsum('bqkgd,bskd->bkgqs', qg, k, preferred_element_type=jnp.float32) * (HEAD_DIM ** -0.5)
    p = jax.nn.softmax(s, axis=-1).astype(v.dtype)
    return jnp.einsum('bkgqs,bskd->bqkgd', p, v)


def latent_attention(q, k_all, v_all):
    b, n = q.shape[:2]
    nblk = n // Q_BLOCK
    qb = q.reshape(b, nblk, Q_BLOCK, N_KV_HEADS, Q_GROUP, HEAD_DIM).transpose(1, 0, 2, 3, 4, 5)
    o = lax.map(lambda qblk: attend(qblk, k_all, v_all), qb)
    return o.transpose(1, 0, 2, 3, 4, 5).reshape(b, n, ATTN_W)


def short_conv3(u, w):
    up = jnp.pad(u, ((0, 0), (1, 1), (0, 0)))
    return up[:, :-2] * w[0] + up[:, 1:-1] * w[1] + up[:, 2:] * w[2]


def split_mix(z):
    cuts = [ATTN_W, ATTN_W + KV_W, ATTN_W + 2 * KV_W,
            ATTN_W + 2 * KV_W + CONV_W, ATTN_W + 2 * KV_W + 2 * CONV_W]
    return jnp.split(z, cuts, axis=-1)


def attn_conv_mixer(a_lat, a_ctx, w_in, q_g, k_g, conv_w, w_out, rope_cos, rope_sin, ctx_live):
    b, n = a_lat.shape[:2]
    nc = a_ctx.shape[1]
    q_l, k_l, v_l, gb_l, gc_l, u_l = split_mix(a_lat @ w_in)
    q_c, k_c, v_c, gb_c, gc_c, u_c = split_mix(a_ctx @ w_in)
    q_l = apply_axial_rope(rms_norm(q_l.reshape(b, n, N_HEADS, HEAD_DIM), q_g), rope_cos, rope_sin)
    k_l = apply_axial_rope(rms_norm(k_l.reshape(b, n, N_KV_HEADS, HEAD_DIM), k_g), rope_cos, rope_sin)
    k_c = rms_norm(k_c.reshape(b, nc, N_KV_HEADS, HEAD_DIM), k_g)
    v_c = v_c.reshape(b, nc, N_KV_HEADS, HEAD_DIM)
    k_all = jnp.concatenate([k_c, k_l], axis=1)
    v_all = jnp.concatenate([v_c, v_l.reshape(b, n, N_KV_HEADS, HEAD_DIM)], axis=1)
    attn_l = latent_attention(q_l, k_all, v_all)
    conv_l = gb_l * short_conv3(gc_l * u_l, conv_w)
    y_lat = jnp.concatenate([attn_l, conv_l], axis=-1) @ w_out
    if not ctx_live:
        return y_lat, None
    q_c = rms_norm(q_c.reshape(b, nc, N_HEADS, HEAD_DIM), q_g)
    attn_c = attend(q_c.reshape(b, nc, N_KV_HEADS, Q_GROUP, HEAD_DIM), k_c, v_c).reshape(b, nc, ATTN_W)
    conv_c = gb_c * short_conv3(gc_c * u_c, conv_w)
    y_ctx = jnp.concatenate([attn_c, conv_c], axis=-1) @ w_out
    return y_lat, y_ctx


def multiscale_pool(h, pool_w, pool_scale):
    b, n, d = h.shape
    hf = h.astype(jnp.float32)
    cs = jnp.concatenate([jnp.zeros((b, 1, d), jnp.float32), jnp.cumsum(hf, axis=1)], axis=1)
    t = jnp.arange(n)
    outs = []
    for gi, w in enumerate(POOL_WINDOWS):
        lo = jnp.clip(t - w // 2, 0, n)
        hi = jnp.clip(t + w - w // 2, 0, n)
        sl = slice(gi * POOL_GROUP_W, (gi + 1) * POOL_GROUP_W)
        csg = cs[..., sl]
        win_sum = jnp.take(csg, hi, axis=1) - jnp.take(csg, lo, axis=1)
        cnt = (hi - lo).astype(jnp.float32)[None, :, None]
        outs.append(win_sum / cnt - hf[..., sl])
    p = jnp.stack(outs, axis=2).astype(h.dtype)
    y = jnp.einsum('bngc,gce->bnge', p, pool_w).reshape(b, n, d)
    return y * pool_scale


def swiglu(h, wg, wu, wd):
    return (jax.nn.silu(h @ wg) * (h @ wu)) @ wd


def moe_top2(h, router_w, router_b, wg, wu, wd):
    logits = (h @ router_w).astype(jnp.float32) + router_b.astype(jnp.float32)
    top_v, top_i = lax.top_k(logits, TOP_K)
    top_p = jax.nn.softmax(top_v, axis=-1)
    gates = jnp.sum(jax.nn.one_hot(top_i, N_EXPERTS, dtype=jnp.float32) * top_p[..., None], axis=-2)
    out = jnp.zeros_like(h)
    for e in range(N_EXPERTS):
        out = out + gates[..., e:e + 1].astype(h.dtype) * swiglu(h, wg[e], wu[e], wd[e])
    return out


def setup_inputs(seed: int = 0) -> dict:
    key = jax.random.key(seed)
    ks = jax.random.split(key, 24)
    f32 = jnp.float32

    def nrm(k, shape, scale):
        return jax.random.normal(k, shape, f32) * scale

    return {
        "x": nrm(ks[0], (BATCH, SEQ, D_MODEL), 1.0),
        "c": nrm(ks[1], (BATCH, D_MODEL), 1.0),
        "ctx": nrm(ks[2], (BATCH, CTX_LEN, D_MODEL), 1.0),
        "c_ctx": nrm(ks[3], (D_MODEL,), 1.0),
        "w_mod": nrm(ks[4], (DEPTH, D_MODEL, N_MOD * D_MODEL), 0.5 * D_MODEL ** -0.5),
        "b_mod": nrm(ks[5], (DEPTH, N_MOD * D_MODEL), 0.02),
        "norm_g": 1.0 + nrm(ks[6], (DEPTH, 2, D_MODEL), 0.05),
        "final_norm_g": 1.0 + nrm(ks[7], (D_MODEL,), 0.05),
        "w_mix_in": nrm(ks[8], (N_EVEN, D_MODEL, MIX_IN_W), D_MODEL ** -0.5),
        "q_norm_g": 1.0 + nrm(ks[9], (N_EVEN, HEAD_DIM), 0.05),
        "k_norm_g": 1.0 + nrm(ks[10], (N_EVEN, HEAD_DIM), 0.05),
        "conv_w": nrm(ks[11], (N_EVEN, 3, CONV_W), 3 ** -0.5),
        "w_mix_out": nrm(ks[12], (N_EVEN, MIX_OUT_IN, D_MODEL), MIX_OUT_IN ** -0.5),
        "ffn_w_gate": nrm(ks[13], (N_EVEN, D_MODEL, D_FF), D_MODEL ** -0.5),
        "ffn_w_up": nrm(ks[14], (N_EVEN, D_MODEL, D_FF), D_MODEL ** -0.5),
        "ffn_w_down": nrm(ks[15], (N_EVEN, D_FF, D_MODEL), D_FF ** -0.5),
        "pool_w": nrm(ks[16], (N_ODD, N_POOL_GROUPS, POOL_GROUP_W, POOL_GROUP_W), POOL_GROUP_W ** -0.5),
        "pool_scale": 1.0 + nrm(ks[17], (N_ODD, D_MODEL), 0.05),
        "router_w": nrm(ks[18], (N_ODD, D_MODEL, N_EXPERTS), D_MODEL ** -0.5),
        "router_b": nrm(ks[19], (N_ODD, N_EXPERTS), 0.01),
        "exp_w_gate": nrm(ks[20], (N_ODD, N_EXPERTS, D_MODEL, D_FF_EXPERT), D_MODEL ** -0.5),
        "exp_w_up": nrm(ks[21], (N_ODD, N_EXPERTS, D_MODEL, D_FF_EXPERT), D_MODEL ** -0.5),
        "exp_w_down": nrm(ks[22], (N_ODD, N_EXPERTS, D_FF_EXPERT, D_MODEL), D_FF_EXPERT ** -0.5),
    }


def reference(x, c, ctx, c_ctx, w_mod, b_mod, norm_g, final_norm_g, w_mix_in, q_norm_g, k_norm_g,
              conv_w, w_mix_out, ffn_w_gate, ffn_w_up, ffn_w_down, pool_w, pool_scale,
              router_w, router_b, exp_w_gate, exp_w_up, exp_w_down):
    n = x.shape[1]
    rope_cos, rope_sin = axial_rope_tables(n)
    x_lat, x_ctx = x, ctx
    for i in range(DEPTH):
        ctx_live = i < DEPTH - 1
        is_even = (i % 2) == 0
        j = i // 2
        sh1, sc1, g1, sh2, sc2, g2 = [m[:, None, :] for m in adaln(c, w_mod[i], b_mod[i])]
        a_lat = modulate(rms_norm(x_lat, norm_g[i, 0]), sh1, sc1)
        need_ctx_in = ctx_live or is_even
        if need_ctx_in:
            csh1, csc1, cg1, csh2, csc2, cg2 = adaln(c_ctx, w_mod[i], b_mod[i])
            a_ctx = modulate(rms_norm(x_ctx, norm_g[i, 0]), csh1, csc1)
        if is_even:
            y_lat, y_ctx = attn_conv_mixer(a_lat, a_ctx, w_mix_in[j], q_norm_g[j], k_norm_g[j], conv_w[j],
                                           w_mix_out[j], rope_cos, rope_sin, ctx_live)
        else:
            y_lat = multiscale_pool(a_lat, pool_w[j], pool_scale[j])
            y_ctx = multiscale_pool(a_ctx, pool_w[j], pool_scale[j]) if ctx_live else None
        x_lat = x_lat + g1 * y_lat
        f_lat = modulate(rms_norm(x_lat, norm_g[i, 1]), sh2, sc2)
        if ctx_live:
            x_ctx = x_ctx + cg1 * y_ctx
            f_ctx = modulate(rms_norm(x_ctx, norm_g[i, 1]), csh2, csc2)
        if is_even:
            x_lat = x_lat + g2 * swiglu(f_lat, ffn_w_gate[j], ffn_w_up[j], ffn_w_down[j])
            if ctx_live:
                x_ctx = x_ctx + cg2 * swiglu(f_ctx, ffn_w_gate[j], ffn_w_up[j], ffn_w_down[j])
        else:
            x_lat = x_lat + g2 * moe_top2(f_lat, router_w[j], router_b[j], exp_w_gate[j], exp_w_up[j], exp_w_down[j])
            if ctx_live:
                x_ctx = x_ctx + cg2 * moe_top2(f_ctx, router_w[j], router_b[j], exp_w_gate[j], exp_w_up[j], exp_w_down[j])
    return rms_norm(x_lat, final_norm_g)
```

```python
import functools
import math

import jax
import jax.numpy as jnp
from jax import lax
from jax.experimental import pallas as pl
from jax.experimental.pallas import tpu as pltpu

F32 = jnp.float32
BF16 = jnp.bfloat16

D_MODEL = 1024
GRID_W = 64
N_HEADS = 8
N_KV_HEADS = 2
HEAD_DIM = 64
ATTN_W = N_HEADS * HEAD_DIM
KV_W = N_KV_HEADS * HEAD_DIM
CONV_W = D_MODEL - ATTN_W
ROPE_AXIS_DIM = HEAD_DIM // 2
ROPE_BASE = 10000.0
POOL_WINDOWS = (2, 4, 8, 16)
POOL_GROUP_W = D_MODEL // len(POOL_WINDOWS)
POOL_HALO = 8
N_EXPERTS = 8
N_MOD = 6
EPS = 1e-6

LANES = 128
SUBLANES = 8
VMEM_LIMIT = 56 * 1024 * 1024

Q_SCALE = (HEAD_DIM ** -0.5) * math.log2(math.e)


def _cparams(*sem):
    return pltpu.CompilerParams(dimension_semantics=sem, vmem_limit_bytes=VMEM_LIMIT)


def _const_spec(shape):
    nd = len(shape)
    return pl.BlockSpec(shape, lambda *_: (0,) * nd, pipeline_mode=pl.Buffered(1))


def _rms_mod(x, g, shift, scale):
    ms = jnp.mean(x * x, axis=-1, keepdims=True)
    return x * lax.rsqrt(ms + EPS) * g * (1.0 + scale) + shift


def _head_rms(z, bd, g):
    sq = z * z
    hi = sq.astype(BF16)
    lo = (sq - hi.astype(F32)).astype(BF16)
    ms = (jnp.dot(hi, bd, preferred_element_type=F32)
          + jnp.dot(lo, bd, preferred_element_type=F32))
    return z * lax.rsqrt(ms + EPS) * g


def _rope128(x, cos, sin_signed):
    lane = lax.broadcasted_iota(jnp.int32, x.shape, 1)
    partner = jnp.where((lane & 16) == 0, pltpu.roll(x, LANES - 16, 1), pltpu.roll(x, 16, 1))
    return x * cos + partner * sin_signed


def _adaln_kernel(c_ref, w_ref, b_ref, o_ref):
    c = c_ref[...]
    s = c * (1.0 / (1.0 + jnp.exp(-c)))
    o_ref[...] = jnp.dot(s, w_ref[...], preferred_element_type=F32,
                         precision=lax.Precision.HIGHEST) + b_ref[...]


def _adaln(cvec, w_mod, b_mod):
    depth, d, n = w_mod.shape
    rows = cvec.shape[0]
    tn = 1536
    return pl.pallas_call(
        _adaln_kernel,
        grid=(depth, n // tn),
        in_specs=[pl.BlockSpec((rows, d), lambda l, j: (0, 0)),
                  pl.BlockSpec((None, d, tn), lambda l, j: (l, 0, j)),
                  pl.BlockSpec((None, 1, tn), lambda l, j: (l, 0, j))],
        out_specs=pl.BlockSpec((None, rows, tn), lambda l, j: (l, 0, j)),
        out_shape=jax.ShapeDtypeStruct((depth, rows, n), F32),
        compiler_params=_cparams("parallel", "parallel"),
        name="adaln",
    )(cvec, w_mod, b_mod.reshape(depth, 1, n))


def _kv_outputs(zk, zv, kdup_ref, vext_ref):
    kt = zk.T
    lane = lax.broadcasted_iota(jnp.int32, zv.shape, 1)
    zv_swapped = pltpu.roll(zv, HEAD_DIM, 1)
    for g in range(N_KV_HEADS):
        kg = kt[g * HEAD_DIM:(g + 1) * HEAD_DIM, :]
        kdup_ref[g] = jnp.concatenate([kg, kg], axis=0).astype(BF16)
        vg = zv if g == 0 else zv_swapped
        vext_ref[g] = jnp.where(lane < HEAD_DIM, vg, 1.0).astype(BF16)


def _l0_in_kernel(x_ref, xp_ref, xn_ref, mod_ref, ng_ref, wq_ref, wk_ref, wv_ref, wc_ref,
                  bd_ref, qg_ref, kg_ref, cw_ref, cos_ref, sin_ref,
                  q_ref, kdup_ref, vext_ref, conv_ref, *, tm):
    i = pl.program_id(1)
    ni = pl.num_programs(1)
    g = ng_ref[...]
    shift, scale = mod_ref[0:1, :], mod_ref[1:2, :]
    a = _rms_mod(x_ref[...], g, shift, scale).astype(BF16)
    cos, sin = cos_ref[...], sin_ref[...]

    zq = jnp.dot(a, wq_ref[...], preferred_element_type=F32)
    qn = _head_rms(zq, bd_ref[...], qg_ref[...])
    for c in range(ATTN_W // LANES):
        sl = slice(c * LANES, (c + 1) * LANES)
        q_ref[:, sl] = (_rope128(qn[:, sl], cos, sin) * Q_SCALE).astype(BF16)

    zk = jnp.dot(a, wk_ref[...], preferred_element_type=F32)
    kn = _rope128(_head_rms(zk, bd_ref[0:KV_W, 0:KV_W], kg_ref[...]), cos, sin)
    zv = jnp.dot(a, wv_ref[...], preferred_element_type=F32)
    _kv_outputs(kn, zv, kdup_ref, vext_ref)

    zc = jnp.dot(a, wc_ref[...], preferred_element_type=F32)
    gb, w = zc[:, 0:CONV_W], zc[:, CONV_W:2 * CONV_W] * zc[:, 2 * CONV_W:3 * CONV_W]
    halo = jnp.concatenate([xp_ref[...], xn_ref[...]], axis=0)
    ah = _rms_mod(halo, g, shift, scale).astype(BF16)
    zh = jnp.dot(ah, wc_ref[:, CONV_W:3 * CONV_W], preferred_element_type=F32)
    wh = zh[:, 0:CONV_W] * zh[:, CONV_W:2 * CONV_W]
    w_before = wh[SUBLANES - 1:SUBLANES, :] * (i > 0).astype(F32)
    w_after = wh[SUBLANES:SUBLANES + 1, :] * (i < ni - 1).astype(F32)
    row = lax.broadcasted_iota(jnp.int32, w.shape, 0)
    w_prev = jnp.where(row == 0, w_before, pltpu.roll(w, 1, 0))
    w_next = jnp.where(row == tm - 1, w_after, pltpu.roll(w, tm - 1, 0))
    y = gb * (w_prev * cw_ref[0:1, :] + w * cw_ref[1:2, :] + w_next * cw_ref[2:3, :])
    conv_ref[...] = y.astype(BF16)


def _l0_in(x, mod, ng, wq, wk, wv, wc, bd, qg, kg, cw, cos, sin, tm):
    b, s, d = x.shape
    nt = s // tm
    hb = tm // SUBLANES
    return pl.pallas_call(
        functools.partial(_l0_in_kernel, tm=tm),
        grid=(b, nt),
        in_specs=[
            pl.BlockSpec((None, tm, d), lambda bi, i: (bi, i, 0)),
            pl.BlockSpec((None, SUBLANES, d), lambda bi, i: (bi, jnp.maximum(i * hb - 1, 0), 0)),
            pl.BlockSpec((None, SUBLANES, d),
                         lambda bi, i: (bi, jnp.minimum((i + 1) * hb, nt * hb - 1), 0)),
            pl.BlockSpec((None, SUBLANES, d), lambda bi, i: (bi, 0, 0)),
            _const_spec((1, d)),
            _const_spec(wq.shape), _const_spec(wk.shape), _const_spec(wv.shape),
            _const_spec(wc.shape), _const_spec(bd.shape),
            _const_spec(qg.shape), _const_spec(kg.shape), _const_spec(cw.shape),
            pl.BlockSpec((tm, LANES), lambda bi, i: (i, 0)),
            pl.BlockSpec((tm, LANES), lambda bi, i: (i, 0)),
        ],
        out_specs=[
            pl.BlockSpec((None, tm, ATTN_W), lambda bi, i: (bi, i, 0)),
            pl.BlockSpec((None, N_KV_HEADS, LANES, tm), lambda bi, i: (bi, 0, 0, i)),
            pl.BlockSpec((None, N_KV_HEADS, tm, LANES), lambda bi, i: (bi, 0, i, 0)),
            pl.BlockSpec((None, tm, CONV_W), lambda bi, i: (bi, i, 0)),
        ],
        out_shape=[
            jax.ShapeDtypeStruct((b, s, ATTN_W), BF16),
            jax.ShapeDtypeStruct((b, N_KV_HEADS, LANES, s), BF16),
            jax.ShapeDtypeStruct((b, N_KV_HEADS, s, LANES), BF16),
            jax.ShapeDtypeStruct((b, s, CONV_W), BF16),
        ],
        compiler_params=_cparams("parallel", "parallel"),
        name="l0_in",
    )(x, x, x, mod, ng, wq, wk, wv, wc, bd, qg, kg, cw, cos, sin)


def _ctx_kv_kernel(x_ref, mod_ref, ng_ref, wk_ref, wv_ref, bd_ref, kg_ref, kdup_ref, vext_ref):
    a = _rms_mod(x_ref[...], ng_ref[...], mod_ref[0:1, :], mod_ref[1:2, :]).astype(BF16)
    zk = jnp.dot(a, wk_ref[...], preferred_element_type=F32)
    kn = _head_rms(zk, bd_ref[0:KV_W, 0:KV_W], kg_ref[...])
    zv = jnp.dot(a, wv_ref[...], preferred_element_type=F32)
    _kv_outputs(kn, zv, kdup_ref, vext_ref)


def _ctx_kv(ctx, mod, ng, wk, wv, bd, kg):
    b, nc, d = ctx.shape
    return pl.pallas_call(
        _ctx_kv_kernel,
        grid=(b,),
        in_specs=[
            pl.BlockSpec((None, nc, d), lambda bi: (bi, 0, 0)),
            _const_spec(mod.shape), _const_spec((1, d)),
            _const_spec(wk.shape), _const_spec(wv.shape), _const_spec(bd.shape),
            _const_spec(kg.shape),
        ],
        out_specs=[
            pl.BlockSpec((None, N_KV_HEADS, LANES, nc), lambda bi: (bi, 0, 0, 0)),
            pl.BlockSpec((None, N_KV_HEADS, nc, LANES), lambda bi: (bi, 0, 0, 0)),
        ],
        out_shape=[
            jax.ShapeDtypeStruct((b, N_KV_HEADS, LANES, nc), BF16),
            jax.ShapeDtypeStruct((b, N_KV_HEADS, nc, LANES), BF16),
        ],
        compiler_params=_cparams("parallel"),
        name="ctx_kv",
    )(ctx, mod, ng, wk, wv, bd, kg)


def _attn_kernel(q_ref, kc_ref, vc_ref, k_ref, v_ref, o_ref, q4_ref, m_ref, acc_ref, *, tq, tk, nk):
    lane = lax.broadcasted_iota(jnp.int32, (tq, LANES), 1)
    is_lo = lane < HEAD_DIM
    zero = jnp.zeros((tq, LANES), BF16)
    for p in range(2):
        qp = q_ref[:, p * LANES:(p + 1) * LANES]
        q4_ref[p * tq:(p + 1) * tq, :] = jnp.where(is_lo, qp, zero)
        q4_ref[(2 + p) * tq:(3 + p) * tq, :] = jnp.where(is_lo, zero, qp)
    m_ref[...] = jnp.full(m_ref.shape, -jnp.inf, F32)
    acc_ref[...] = jnp.zeros(acc_ref.shape, F32)

    def step(kt, vx):
        s = jnp.dot(q4_ref[...], kt, preferred_element_type=F32)
        m_prev = m_ref[...]
        m_new = jnp.maximum(m_prev, jnp.max(s, axis=-1, keepdims=True))
        alpha = jnp.exp2(m_prev - m_new)
        p = jnp.exp2(s - m_new[:, 0:1])
        acc_ref[...] = alpha * acc_ref[...] + jnp.dot(p.astype(BF16), vx, preferred_element_type=F32)
        m_ref[...] = m_new

    step(kc_ref[...], vc_ref[...])

    def body(c, carry):
        off = pl.multiple_of(c * tk, tk)
        step(k_ref[:, pl.ds(off, tk)], v_ref[pl.ds(off, tk), :])
        return carry

    lax.fori_loop(0, nk, body, 0)

    acc = acc_ref[...]
    r = acc / pltpu.roll(acc, HEAD_DIM, 1)
    for p in range(2):
        even = r[p * tq:(p + 1) * tq, :]
        odd = pltpu.roll(r[(2 + p) * tq:(3 + p) * tq, :], HEAD_DIM, 1)
        o_ref[:, p * LANES:(p + 1) * LANES] = jnp.where(is_lo, even, odd).astype(BF16)


def _attention(q, kdup_c, vext_c, kdup, vext, tq, tk):
    b, s, _ = q.shape
    nc = kdup_c.shape[-1]
    gw = ATTN_W // N_KV_HEADS
    rows = (gw // HEAD_DIM) * tq
    return pl.pallas_call(
        functools.partial(_attn_kernel, tq=tq, tk=tk, nk=s // tk),
        grid=(b, N_KV_HEADS, s // tq),
        in_specs=[
            pl.BlockSpec((None, tq, gw), lambda bi, g, i: (bi, i, g)),
            pl.BlockSpec((None, None, LANES, nc), lambda bi, g, i: (bi, g, 0, 0)),
            pl.BlockSpec((None, None, nc, LANES), lambda bi, g, i: (bi, g, 0, 0)),
            pl.BlockSpec((None, None, LANES, s), lambda bi, g, i: (bi, g, 0, 0)),
            pl.BlockSpec((None, None, s, LANES), lambda bi, g, i: (bi, g, 0, 0)),
        ],
        out_specs=pl.BlockSpec((None, tq, gw), lambda bi, g, i: (bi, i, g)),
        out_shape=jax.ShapeDtypeStruct((b, s, ATTN_W), BF16),
        scratch_shapes=[pltpu.VMEM((rows, LANES), BF16),
                        pltpu.VMEM((rows, LANES), F32),
                        pltpu.VMEM((rows, LANES), F32)],
        compiler_params=_cparams("parallel", "parallel", "parallel"),
        name="attention",
    )(q, kdup_c, vext_c, kdup, vext)


def _l0_out_kernel(x_ref, attn_ref, conv_ref, mod_ref, ng_ref, wa_ref, wc_ref, x1_ref, f_ref):
    y = (jnp.dot(attn_ref[...], wa_ref[...], preferred_element_type=F32)
         + jnp.dot(conv_ref[...], wc_ref[...], preferred_element_type=F32))
    x1 = x_ref[...] + mod_ref[2:3, :] * y
    x1_ref[...] = x1
    f_ref[...] = _rms_mod(x1, ng_ref[...], mod_ref[3:4, :], mod_ref[4:5, :]).astype(BF16)


def _l0_out(x, attn, conv, mod, ng, wa, wc, tm):
    b, s, d = x.shape
    tile = lambda w: pl.BlockSpec((None, tm, w), lambda bi, i: (bi, i, 0))
    return pl.pallas_call(
        _l0_out_kernel,
        grid=(b, s // tm),
        in_specs=[tile(d), tile(ATTN_W), tile(CONV_W),
                  pl.BlockSpec((None, SUBLANES, d), lambda bi, i: (bi, 0, 0)),
                  _const_spec((1, d)), _const_spec(wa.shape), _const_spec(wc.shape)],
        out_specs=[tile(d), tile(d)],
        out_shape=[jax.ShapeDtypeStruct((b, s, d), F32), jax.ShapeDtypeStruct((b, s, d), BF16)],
        compiler_params=_cparams("parallel", "parallel"),
        name="l0_out",
    )(x, attn, conv, mod, ng, wa, wc)


def _ffn_kernel(x_ref, f_ref, mod_ref, wg_ref, wu_ref, wd_ref, o_ref, *, n_chunks):
    f = f_ref[...]
    acc = None
    for c in range(n_chunks):
        hg = jnp.dot(f, wg_ref[c], preferred_element_type=F32)
        hu = jnp.dot(f, wu_ref[c], preferred_element_type=F32)
        h = (hg * (1.0 / (1.0 + jnp.exp(-hg))) * hu).astype(BF16)
        part = jnp.dot(h, wd_ref[c], preferred_element_type=F32)
        acc = part if acc is None else acc + part
    o_ref[...] = x_ref[...] + mod_ref[5:6, :] * acc


def _ffn(x1, f, mod, wg, wu, wd, tm):
    b, s, d = x1.shape
    tile = lambda: pl.BlockSpec((None, tm, d), lambda bi, i: (bi, i, 0))
    return pl.pallas_call(
        functools.partial(_ffn_kernel, n_chunks=wg.shape[0]),
        grid=(b, s // tm),
        in_specs=[tile(), tile(),
                  pl.BlockSpec((None, SUBLANES, d), lambda bi, i: (bi, 0, 0)),
                  _const_spec(wg.shape), _const_spec(wu.shape), _const_spec(wd.shape)],
        out_specs=tile(),
        out_shape=jax.ShapeDtypeStruct((b, s, d), F32),
        compiler_params=_cparams("parallel", "parallel"),
        name="ffn",
    )(x1, f, mod, wg, wu, wd)


def _pool_kernel(x_ref, xp_ref, xn_ref, mod_ref, ng1_ref, ng2_ref, pw_ref, ps_ref, rw_ref, rb_ref,
                 x3_ref, f_ref, gates_ref, *, tm, seq):
    i = pl.program_id(1)
    ni = pl.num_programs(1)
    g = ng1_ref[...]
    shift, scale = mod_ref[0:1, :], mod_ref[1:2, :]
    x = x_ref[...]
    a = _rms_mod(x, g, shift, scale)
    a_before = _rms_mod(xp_ref[...], g, shift, scale) * (i > 0).astype(F32)
    a_after = _rms_mod(xn_ref[...], g, shift, scale) * (i < ni - 1).astype(F32)
    ext = jnp.concatenate([a_before, a, a_after], axis=0)
    n_ext = tm + 2 * POOL_HALO

    t = i * tm + lax.broadcasted_iota(jnp.int32, (tm, 1), 0)
    ys = []
    for gi, w in enumerate(POOL_WINDOWS):
        sl = slice(gi * POOL_GROUP_W, (gi + 1) * POOL_GROUP_W)
        e = ext[:, sl]
        acc = e + pltpu.roll(e, 1, 0)
        half = 1
        while 2 * half < w:
            acc = pltpu.roll(acc, n_ext - half, 0) + pltpu.roll(acc, half, 0)
            half *= 2
        win = acc[POOL_HALO:POOL_HALO + tm, :]
        cnt = jnp.minimum(t + w // 2, seq) - jnp.maximum(t - w // 2, 0)
        p = (win / cnt.astype(F32) - a[:, sl]).astype(BF16)
        ys.append(jnp.dot(p, pw_ref[gi], preferred_element_type=F32))
    y = jnp.concatenate(ys, axis=1) * ps_ref[...]

    x3 = x + mod_ref[2:3, :] * y
    x3_ref[...] = x3
    f = _rms_mod(x3, ng2_ref[...], mod_ref[3:4, :], mod_ref[4:5, :])
    f_ref[...] = f.astype(BF16)

    logits = jnp.dot(f, rw_ref[...], preferred_element_type=F32,
                     precision=lax.Precision.HIGHEST) + rb_ref[...]
    lane = lax.broadcasted_iota(jnp.int32, logits.shape, 1)
    neg = jnp.float32(-jnp.inf)
    logits = jnp.where(lane < N_EXPERTS, logits, neg)
    v1 = jnp.max(logits, axis=-1, keepdims=True)
    i1 = jnp.min(jnp.where(logits == v1, lane, LANES), axis=-1, keepdims=True)
    rest = jnp.where(lane == i1, neg, logits)
    v2 = jnp.max(rest, axis=-1, keepdims=True)
    i2 = jnp.min(jnp.where(rest == v2, lane, LANES), axis=-1, keepdims=True)
    e2 = jnp.exp(v2 - v1)
    p1 = 1.0 / (1.0 + e2)
    p2 = e2 / (1.0 + e2)
    gates_ref[...] = jnp.where(lane == i1, p1, jnp.where(lane == i2, p2, 0.0))


def _pool_router(x2, mod, ng1, ng2, pw, ps, rw, rb, tm):
    b, s, d = x2.shape
    nt = s // tm
    hb = tm // POOL_HALO
    tile = lambda w: pl.BlockSpec((None, tm, w), lambda bi, i: (bi, i, 0))
    return pl.pallas_call(
        functools.partial(_pool_kernel, tm=tm, seq=s),
        grid=(b, nt),
        in_specs=[
            tile(d),
            pl.BlockSpec((None, POOL_HALO, d), lambda bi, i: (bi, jnp.maximum(i * hb - 1, 0), 0)),
            pl.BlockSpec((None, POOL_HALO, d),
                         lambda bi, i: (bi, jnp.minimum((i + 1) * hb, nt * hb - 1), 0)),
            pl.BlockSpec((None, SUBLANES, d), lambda bi, i: (bi, 0, 0)),
            _const_spec((1, d)), _const_spec((1, d)), _const_spec(pw.shape), _const_spec((1, d)),
            _const_spec(rw.shape), _const_spec(rb.shape),
        ],
        out_specs=[tile(d), tile(d), tile(LANES)],
        out_shape=[jax.ShapeDtypeStruct((b, s, d), F32), jax.ShapeDtypeStruct((b, s, d), BF16),
                   jax.ShapeDtypeStruct((b, s, LANES), F32)],
        compiler_params=_cparams("parallel", "parallel"),
        name="pool_router",
    )(x2, x2, x2, mod, ng1, ng2, pw, ps, rw, rb)


def _moe_kernel(x_ref, f_ref, gates_ref, mod_ref, fg_ref, wg_ref, wu_ref, wd_ref, o_ref, acc_ref):
    e = pl.program_id(2)

    @pl.when(e == 0)
    def _():
        acc_ref[...] = jnp.zeros(acc_ref.shape, F32)

    gates = gates_ref[...]
    lane = lax.broadcasted_iota(jnp.int32, gates.shape, 1)
    gate = jnp.sum(jnp.where(lane == e, gates, 0.0), axis=-1, keepdims=True)
    f = f_ref[...]
    hg = jnp.dot(f, wg_ref[...], preferred_element_type=F32)
    hu = jnp.dot(f, wu_ref[...], preferred_element_type=F32)
    h = (hg * (1.0 / (1.0 + jnp.exp(-hg))) * hu).astype(BF16)
    acc_ref[...] += gate * jnp.dot(h, wd_ref[...], preferred_element_type=F32)

    @pl.when(e == pl.num_programs(2) - 1)
    def _():
        x4 = x_ref[...] + mod_ref[5:6, :] * acc_ref[...]
        ms = jnp.mean(x4 * x4, axis=-1, keepdims=True)
        o_ref[...] = x4 * lax.rsqrt(ms + EPS) * fg_ref[...]


def _moe_dense(x3, f, gates, mod, fg, wg, wu, wd, tm):
    b, s, d = x3.shape
    ne, _, dff = wg.shape
    tile = lambda w: pl.BlockSpec((None, tm, w), lambda bi, i, e: (bi, i, 0))
    return pl.pallas_call(
        _moe_kernel,
        grid=(b, s // tm, ne),
        in_specs=[tile(d), tile(d), tile(LANES),
                  pl.BlockSpec((None, SUBLANES, d), lambda bi, i, e: (bi, 0, 0)),
                  pl.BlockSpec((1, d), lambda bi, i, e: (0, 0)),
                  pl.BlockSpec((None, d, dff), lambda bi, i, e: (e, 0, 0)),
                  pl.BlockSpec((None, d, dff), lambda bi, i, e: (e, 0, 0)),
                  pl.BlockSpec((None, dff, d), lambda bi, i, e: (e, 0, 0))],
        out_specs=tile(d),
        out_shape=jax.ShapeDtypeStruct((b, s, d), F32),
        scratch_shapes=[pltpu.VMEM((tm, d), F32)],
        compiler_params=_cparams("parallel", "parallel", "arbitrary"),
        name="moe",
    )(x3, f, gates, mod, fg, wg, wu, wd)


def _rope_tables(n):
    t = jnp.arange(n)
    pos = jnp.stack([t // GRID_W, t % GRID_W], axis=1).astype(F32)
    freqs = ROPE_BASE ** (-jnp.arange(0, ROPE_AXIS_DIM, 2, dtype=F32) / ROPE_AXIS_DIM)
    ang = pos[:, :, None] * freqs
    ang = jnp.broadcast_to(ang[:, :, None, :], (n, 2, 2, ROPE_AXIS_DIM // 2))
    sign = jnp.array([-1.0, 1.0], F32)[None, None, :, None]
    cos = jnp.cos(ang).reshape(n, HEAD_DIM)
    sin = (jnp.sin(ang) * sign).reshape(n, HEAD_DIM)
    return jnp.tile(cos, (1, LANES // HEAD_DIM)), jnp.tile(sin, (1, LANES // HEAD_DIM))


def _pad_rows(m, rows):
    return jnp.pad(m, ((0, 0),) * (m.ndim - 2) + ((0, rows - m.shape[-2]), (0, 0)))


def kernel(x, c, ctx, c_ctx, w_mod, b_mod, norm_g, final_norm_g, w_mix_in, q_norm_g, k_norm_g, conv_w, w_mix_out, ffn_w_gate, ffn_w_up, ffn_w_down, pool_w, pool_scale, router_w, router_b, exp_w_gate, exp_w_up, exp_w_down):
    b, s, d = x.shape
    tm = min(512, s)
    tq = min(256, s)
    tk = min(512, s)

    rows = -(-(b + 1) // SUBLANES) * SUBLANES
    cvec = _pad_rows(jnp.concatenate([c, c_ctx[None, :]], axis=0), rows)
    mods = _adaln(cvec, w_mod, b_mod).reshape(w_mod.shape[0], rows, N_MOD, d)
    mod_lat = [_pad_rows(mods[l, :b], SUBLANES) for l in range(2)]
    mod_ctx0 = _pad_rows(mods[0, b], SUBLANES)

    w_in = w_mix_in[0].astype(BF16)
    cuts = (ATTN_W, ATTN_W + KV_W, ATTN_W + 2 * KV_W)
    wq, wk, wv, wc = w_in[:, :cuts[0]], w_in[:, cuts[0]:cuts[1]], w_in[:, cuts[1]:cuts[2]], w_in[:, cuts[2]:]
    head = jnp.arange(ATTN_W) // HEAD_DIM
    bd = ((head[:, None] == head[None, :]).astype(F32) / HEAD_DIM).astype(BF16)
    qg = jnp.tile(q_norm_g[0], N_HEADS)[None, :]
    kg = jnp.tile(k_norm_g[0], N_KV_HEADS)[None, :]
    cw = _pad_rows(conv_w[0], SUBLANES)
    cos, sin = _rope_tables(s)
    ng0a, ng0b = norm_g[0, 0][None, :], norm_g[0, 1][None, :]

    q, kdup, vext, conv = _l0_in(x, mod_lat[0], ng0a, wq, wk, wv, wc, bd, qg, kg, cw, cos, sin, tm)
    kdup_c, vext_c = _ctx_kv(ctx, mod_ctx0, ng0a, wk, wv, bd, kg)
    attn = _attention(q, kdup_c, vext_c, kdup, vext, tq, tk)

    w_out = w_mix_out[0].astype(BF16)
    x1, f1 = _l0_out(x, attn, conv, mod_lat[0], ng0b, w_out[:ATTN_W], w_out[ATTN_W:], tm)
    n_chunks = 2
    dff = ffn_w_gate.shape[-1]
    split_cols = lambda w: w.astype(BF16).reshape(d, n_chunks, dff // n_chunks).transpose(1, 0, 2)
    wd0 = ffn_w_down[0].astype(BF16).reshape(n_chunks, dff // n_chunks, d)
    x2 = _ffn(x1, f1, mod_lat[0], split_cols(ffn_w_gate[0]), split_cols(ffn_w_up[0]), wd0, tm)

    rw = jnp.pad(router_w[0], ((0, 0), (0, LANES - N_EXPERTS)))
    rb = jnp.pad(router_b[0], (0, LANES - N_EXPERTS))[None, :]
    x3, f2, gates = _pool_router(x2, mod_lat[1], norm_g[1, 0][None, :], norm_g[1, 1][None, :],
                                 pool_w[0].astype(BF16), pool_scale[0][None, :], rw, rb, tm)
    return _moe_dense(x3, f2, gates, mod_lat[1], final_norm_g[None, :],
                      exp_w_gate[0].astype(BF16), exp_w_up[0].astype(BF16), exp_w_down[0].astype(BF16),
                      tm)
```

```python
import functools
import math

import jax
import jax.numpy as jnp
from jax import lax
from jax.experimental import pallas as pl
from jax.experimental.pallas import tpu as pltpu

F32 = jnp.float32
BF16 = jnp.bfloat16

D_MODEL = 1024
GRID_W = 64
N_HEADS = 8
N_KV_HEADS = 2
HEAD_DIM = 64
ATTN_W = N_HEADS * HEAD_DIM
KV_W = N_KV_HEADS * HEAD_DIM
CONV_W = D_MODEL - ATTN_W
ROPE_AXIS_DIM = HEAD_DIM // 2
ROPE_BASE = 10000.0
POOL_WINDOWS = (2, 4, 8, 16)
POOL_GROUP_W = D_MODEL // len(POOL_WINDOWS)
POOL_HALO = 8
N_EXPERTS = 8
DISPATCH_T = 256
SEG_ALIGN = 16
DISPATCH_ROWS = 2 * DISPATCH_T + N_EXPERTS * SEG_ALIGN
EXPERT_RB = 512
N_MOD = 6
EPS = 1e-6

LANES = 128
SUBLANES = 8
VMEM_LIMIT = 56 * 1024 * 1024

Q_SCALE = (HEAD_DIM ** -0.5) * math.log2(math.e)
SHIFT_MARGIN = 1.01
MAX_FIXED_SHIFT = 48.0


def _cparams(*sem):
    return pltpu.CompilerParams(dimension_semantics=sem, vmem_limit_bytes=VMEM_LIMIT)


def _const_spec(shape):
    nd = len(shape)
    return pl.BlockSpec(shape, lambda *_: (0,) * nd, pipeline_mode=pl.Buffered(1))


def _rms_mod(x, g, shift, scale):
    ms = jnp.mean(x * x, axis=-1, keepdims=True)
    return x * lax.rsqrt(ms + EPS) * g * (1.0 + scale) + shift


def _head_rms(z, bd, g):
    sq = z * z
    hi = sq.astype(BF16)
    lo = (sq - hi.astype(F32)).astype(BF16)
    ms = (jnp.dot(hi, bd, preferred_element_type=F32)
          + jnp.dot(lo, bd, preferred_element_type=F32))
    return z * lax.rsqrt(ms + EPS) * g


def _rope128(x, cos, sin_signed):
    lane = lax.broadcasted_iota(jnp.int32, x.shape, 1)
    partner = jnp.where((lane & 16) == 0, pltpu.roll(x, LANES - 16, 1), pltpu.roll(x, 16, 1))
    return x * cos + partner * sin_signed


def _adaln_kernel(c_ref, w_ref, b_ref, o_ref):
    c = c_ref[...]
    s = c * (1.0 / (1.0 + jnp.exp(-c)))
    o_ref[...] = jnp.dot(s, w_ref[...], preferred_element_type=F32,
                         precision=lax.Precision.HIGHEST) + b_ref[...]


def _adaln(cvec, w_mod, b_mod):
    depth, d, n = w_mod.shape
    rows = cvec.shape[0]
    tn = 1536
    return pl.pallas_call(
        _adaln_kernel,
        grid=(depth, n // tn),
        in_specs=[pl.BlockSpec((rows, d), lambda l, j: (0, 0)),
                  pl.BlockSpec((None, d, tn), lambda l, j: (l, 0, j)),
                  pl.BlockSpec((None, 1, tn), lambda l, j: (l, 0, j))],
        out_specs=pl.BlockSpec((None, rows, tn), lambda l, j: (l, 0, j)),
        out_shape=jax.ShapeDtypeStruct((depth, rows, n), F32),
        compiler_params=_cparams("parallel", "parallel"),
        name="adaln",
    )(cvec, w_mod, b_mod.reshape(depth, 1, n))


def _kv_outputs(zk, zv, kext_ref, vext_ref):
    kt = zk.T
    rows = zk.shape[0]
    lane = lax.broadcasted_iota(jnp.int32, zv.shape, 1)
    sub = lax.broadcasted_iota(jnp.int32, (LANES - HEAD_DIM, rows), 0)
    ones_row = jnp.where(sub == 0, 1.0, 0.0)
    zv_swapped = pltpu.roll(zv, HEAD_DIM, 1)
    for g in range(N_KV_HEADS):
        kg = kt[g * HEAD_DIM:(g + 1) * HEAD_DIM, :]
        kext_ref[g] = jnp.concatenate([kg, ones_row], axis=0).astype(BF16)
        vg = zv if g == 0 else zv_swapped
        vext_ref[g] = jnp.where(lane < HEAD_DIM, vg, 1.0).astype(BF16)


def _l0_in_kernel(x_ref, xp_ref, xn_ref, mod_ref, ng_ref, wq_ref, wk_ref, wv_ref, wc_ref,
                  bd_ref, qg_ref, kg_ref, cw_ref, cos_ref, sin_ref, qb_ref,
                  q_ref, kext_ref, vext_ref, conv_ref, *, tm):
    i = pl.program_id(1)
    ni = pl.num_programs(1)
    g = ng_ref[...]
    shift, scale = mod_ref[0:1, :], mod_ref[1:2, :]
    a = _rms_mod(x_ref[...], g, shift, scale).astype(BF16)
    cos, sin = cos_ref[...], sin_ref[...]

    zq = jnp.dot(a, wq_ref[...], preferred_element_type=F32)
    qn = _head_rms(zq, bd_ref[...], qg_ref[...])
    lane = lax.broadcasted_iota(jnp.int32, (tm, LANES), 1)
    is_lo = lane < HEAD_DIM
    qbias = qb_ref[...]
    for c in range(ATTN_W // LANES):
        r = _rope128(qn[:, c * LANES:(c + 1) * LANES], cos, sin) * Q_SCALE
        q_ref[:, (2 * c) * LANES:(2 * c + 1) * LANES] = jnp.where(is_lo, r, qbias).astype(BF16)
        q_ref[:, (2 * c + 1) * LANES:(2 * c + 2) * LANES] = jnp.where(
            is_lo, pltpu.roll(r, HEAD_DIM, 1), qbias).astype(BF16)

    zk = jnp.dot(a, wk_ref[...], preferred_element_type=F32)
    kn = _rope128(_head_rms(zk, bd_ref[0:KV_W, 0:KV_W], kg_ref[...]), cos, sin)
    zv = jnp.dot(a, wv_ref[...], preferred_element_type=F32)
    _kv_outputs(kn, zv, kext_ref, vext_ref)

    zc = jnp.dot(a, wc_ref[...], preferred_element_type=F32)
    gb, w = zc[:, 0:CONV_W], zc[:, CONV_W:2 * CONV_W] * zc[:, 2 * CONV_W:3 * CONV_W]
    halo = jnp.concatenate([xp_ref[...], xn_ref[...]], axis=0)
    ah = _rms_mod(halo, g, shift, scale).astype(BF16)
    zh = jnp.dot(ah, wc_ref[:, CONV_W:3 * CONV_W], preferred_element_type=F32)
    wh = zh[:, 0:CONV_W] * zh[:, CONV_W:2 * CONV_W]
    w_before = wh[SUBLANES - 1:SUBLANES, :] * (i > 0).astype(F32)
    w_after = wh[SUBLANES:SUBLANES + 1, :] * (i < ni - 1).astype(F32)
    row = lax.broadcasted_iota(jnp.int32, w.shape, 0)
    w_prev = jnp.where(row == 0, w_before, pltpu.roll(w, 1, 0))
    w_next = jnp.where(row == tm - 1, w_after, pltpu.roll(w, tm - 1, 0))
    y = gb * (w_prev * cw_ref[0:1, :] + w * cw_ref[1:2, :] + w_next * cw_ref[2:3, :])
    conv_ref[...] = y.astype(BF16)


def _l0_in(x, mod, ng, wq, wk, wv, wc, bd, qg, kg, cw, cos, sin, qbias, tm):
    b, s, d = x.shape
    nt = s // tm
    hb = tm // SUBLANES
    return pl.pallas_call(
        functools.partial(_l0_in_kernel, tm=tm),
        grid=(b, nt),
        in_specs=[
            pl.BlockSpec((None, tm, d), lambda bi, i: (bi, i, 0)),
            pl.BlockSpec((None, SUBLANES, d), lambda bi, i: (bi, jnp.maximum(i * hb - 1, 0), 0)),
            pl.BlockSpec((None, SUBLANES, d),
                         lambda bi, i: (bi, jnp.minimum((i + 1) * hb, nt * hb - 1), 0)),
            pl.BlockSpec((None, SUBLANES, d), lambda bi, i: (bi, 0, 0)),
            _const_spec((1, d)),
            _const_spec(wq.shape), _const_spec(wk.shape), _const_spec(wv.shape),
            _const_spec(wc.shape), _const_spec(bd.shape),
            _const_spec(qg.shape), _const_spec(kg.shape), _const_spec(cw.shape),
            pl.BlockSpec((tm, LANES), lambda bi, i: (i, 0)),
            pl.BlockSpec((tm, LANES), lambda bi, i: (i, 0)),
            _const_spec(qbias.shape),
        ],
        out_specs=[
            pl.BlockSpec((None, tm, N_HEADS * LANES), lambda bi, i: (bi, i, 0)),
            pl.BlockSpec((None, N_KV_HEADS, LANES, tm), lambda bi, i: (bi, 0, 0, i)),
            pl.BlockSpec((None, N_KV_HEADS, tm, LANES), lambda bi, i: (bi, 0, i, 0)),
            pl.BlockSpec((None, tm, CONV_W), lambda bi, i: (bi, i, 0)),
        ],
        out_shape=[
            jax.ShapeDtypeStruct((b, s, N_HEADS * LANES), BF16),
            jax.ShapeDtypeStruct((b, N_KV_HEADS, LANES, s), BF16),
            jax.ShapeDtypeStruct((b, N_KV_HEADS, s, LANES), BF16),
            jax.ShapeDtypeStruct((b, s, CONV_W), BF16),
        ],
        compiler_params=_cparams("parallel", "parallel"),
        name="l0_in",
    )(x, x, x, mod, ng, wq, wk, wv, wc, bd, qg, kg, cw, cos, sin, qbias)


def _ctx_kv_kernel(x_ref, mod_ref, ng_ref, wk_ref, wv_ref, bd_ref, kg_ref, kext_ref, vext_ref):
    a = _rms_mod(x_ref[...], ng_ref[...], mod_ref[0:1, :], mod_ref[1:2, :]).astype(BF16)
    zk = jnp.dot(a, wk_ref[...], preferred_element_type=F32)
    kn = _head_rms(zk, bd_ref[0:KV_W, 0:KV_W], kg_ref[...])
    zv = jnp.dot(a, wv_ref[...], preferred_element_type=F32)
    _kv_outputs(kn, zv, kext_ref, vext_ref)


def _ctx_kv(ctx, mod, ng, wk, wv, bd, kg):
    b, nc, d = ctx.shape
    return pl.pallas_call(
        _ctx_kv_kernel,
        grid=(b,),
        in_specs=[
            pl.BlockSpec((None, nc, d), lambda bi: (bi, 0, 0)),
            _const_spec(mod.shape), _const_spec((1, d)),
            _const_spec(wk.shape), _const_spec(wv.shape), _const_spec(bd.shape),
            _const_spec(kg.shape),
        ],
        out_specs=[
            pl.BlockSpec((None, N_KV_HEADS, LANES, nc), lambda bi: (bi, 0, 0, 0)),
            pl.BlockSpec((None, N_KV_HEADS, nc, LANES), lambda bi: (bi, 0, 0, 0)),
        ],
        out_shape=[
            jax.ShapeDtypeStruct((b, N_KV_HEADS, LANES, nc), BF16),
            jax.ShapeDtypeStruct((b, N_KV_HEADS, nc, LANES), BF16),
        ],
        compiler_params=_cparams("parallel"),
        name="ctx_kv",
    )(ctx, mod, ng, wk, wv, bd, kg)


def _attn_kernel(q_ref, kc_ref, vc_ref, k_ref, v_ref, o_ref, qs_ref, acc_ref, *m_scratch,
                 tq, tk, nk, online):
    n_h = q_ref.shape[1] // LANES
    for h in range(n_h):
        qs_ref[h * tq:(h + 1) * tq, :] = q_ref[:, h * LANES:(h + 1) * LANES]

    if online:
        m_ref, = m_scratch
        m_ref[...] = jnp.full(m_ref.shape, -jnp.inf, F32)
        acc_ref[...] = jnp.zeros(acc_ref.shape, F32)

    def step(kt, vx, first):
        s = jnp.dot(qs_ref[...], kt, preferred_element_type=F32)
        if online:
            m_prev = m_ref[...]
            m_new = jnp.maximum(m_prev, jnp.max(s, axis=-1, keepdims=True))
            p = jnp.exp2(s - m_new[:, 0:1]).astype(BF16)
            acc_ref[...] = (jnp.exp2(m_prev - m_new) * acc_ref[...]
                            + jnp.dot(p, vx, preferred_element_type=F32))
            m_ref[...] = m_new
        else:
            pv = jnp.dot(jnp.exp2(s).astype(BF16), vx, preferred_element_type=F32)
            if first:
                acc_ref[...] = pv
            else:
                acc_ref[...] += pv

    step(kc_ref[...], vc_ref[...], True)

    def body(c, carry):
        off = pl.multiple_of(c * tk, tk)
        step(k_ref[:, pl.ds(off, tk)], v_ref[pl.ds(off, tk), :], False)
        return carry

    lax.fori_loop(0, nk, body, 0)

    acc = acc_ref[...]
    r = acc / pltpu.roll(acc, HEAD_DIM, 1)
    is_lo = lax.broadcasted_iota(jnp.int32, (tq, LANES), 1) < HEAD_DIM
    for p in range(n_h // 2):
        even = r[(2 * p) * tq:(2 * p + 1) * tq, :]
        odd = pltpu.roll(r[(2 * p + 1) * tq:(2 * p + 2) * tq, :], HEAD_DIM, 1)
        o_ref[:, p * LANES:(p + 1) * LANES] = jnp.where(is_lo, even, odd).astype(BF16)


def _attention(q, kext_c, vext_c, kext, vext, tq, tk, online):
    b, s, _ = q.shape
    nc = kext_c.shape[-1]
    n_h = N_HEADS // N_KV_HEADS
    rows = n_h * tq
    scratch = [pltpu.VMEM((rows, LANES), BF16), pltpu.VMEM((rows, LANES), F32)]
    if online:
        scratch.append(pltpu.VMEM((rows, LANES), F32))
    return pl.pallas_call(
        functools.partial(_attn_kernel, tq=tq, tk=tk, nk=s // tk, online=online),
        grid=(b, N_KV_HEADS, s // tq),
        in_specs=[
            pl.BlockSpec((None, tq, n_h * LANES), lambda bi, g, i: (bi, i, g)),
            pl.BlockSpec((None, None, LANES, nc), lambda bi, g, i: (bi, g, 0, 0)),
            pl.BlockSpec((None, None, nc, LANES), lambda bi, g, i: (bi, g, 0, 0)),
            pl.BlockSpec((None, None, LANES, s), lambda bi, g, i: (bi, g, 0, 0)),
            pl.BlockSpec((None, None, s, LANES), lambda bi, g, i: (bi, g, 0, 0)),
        ],
        out_specs=pl.BlockSpec((None, tq, n_h * HEAD_DIM), lambda bi, g, i: (bi, i, g)),
        out_shape=jax.ShapeDtypeStruct((b, s, ATTN_W), BF16),
        scratch_shapes=scratch,
        compiler_params=_cparams("parallel", "parallel", "parallel"),
        name="attention_online" if online else "attention",
    )(q, kext_c, vext_c, kext, vext)


def _l0_out_kernel(x_ref, attn_ref, conv_ref, mod_ref, ng_ref, wa_ref, wc_ref, x1_ref, f_ref):
    y = (jnp.dot(attn_ref[...], wa_ref[...], preferred_element_type=F32)
         + jnp.dot(conv_ref[...], wc_ref[...], preferred_element_type=F32))
    x1 = x_ref[...] + mod_ref[2:3, :] * y
    x1_ref[...] = x1
    f_ref[...] = _rms_mod(x1, ng_ref[...], mod_ref[3:4, :], mod_ref[4:5, :]).astype(BF16)


def _l0_out(x, attn, conv, mod, ng, wa, wc, tm):
    b, s, d = x.shape
    tile = lambda w: pl.BlockSpec((None, tm, w), lambda bi, i: (bi, i, 0))
    return pl.pallas_call(
        _l0_out_kernel,
        grid=(b, s // tm),
        in_specs=[tile(d), tile(ATTN_W), tile(CONV_W),
                  pl.BlockSpec((None, SUBLANES, d), lambda bi, i: (bi, 0, 0)),
                  _const_spec((1, d)), _const_spec(wa.shape), _const_spec(wc.shape)],
        out_specs=[tile(d), tile(d)],
        out_shape=[jax.ShapeDtypeStruct((b, s, d), F32), jax.ShapeDtypeStruct((b, s, d), BF16)],
        compiler_params=_cparams("parallel", "parallel"),
        name="l0_out",
    )(x, attn, conv, mod, ng, wa, wc)


def _ffn_kernel(x_ref, f_ref, mod_ref, wg_ref, wu_ref, wd_ref, o_ref, *, n_chunks):
    f = f_ref[...]
    acc = None
    for c in range(n_chunks):
        hg = jnp.dot(f, wg_ref[c], preferred_element_type=F32)
        hu = jnp.dot(f, wu_ref[c], preferred_element_type=F32)
        h = (hg * (1.0 / (1.0 + jnp.exp(-hg))) * hu).astype(BF16)
        part = jnp.dot(h, wd_ref[c], preferred_element_type=F32)
        acc = part if acc is None else acc + part
    o_ref[...] = x_ref[...] + mod_ref[5:6, :] * acc


def _ffn(x1, f, mod, wg, wu, wd, tm):
    b, s, d = x1.shape
    tile = lambda: pl.BlockSpec((None, tm, d), lambda bi, i: (bi, i, 0))
    return pl.pallas_call(
        functools.partial(_ffn_kernel, n_chunks=wg.shape[0]),
        grid=(b, s // tm),
        in_specs=[tile(), tile(),
                  pl.BlockSpec((None, SUBLANES, d), lambda bi, i: (bi, 0, 0)),
                  _const_spec(wg.shape), _const_spec(wu.shape), _const_spec(wd.shape)],
        out_specs=tile(),
        out_shape=jax.ShapeDtypeStruct((b, s, d), F32),
        compiler_params=_cparams("parallel", "parallel"),
        name="ffn",
    )(x1, f, mod, wg, wu, wd)


def _pool_kernel(x_ref, xp_ref, xn_ref, mod_ref, ng1_ref, ng2_ref, pw_ref, ps_ref, rwh_ref, rwl_ref, rb_ref,
                 x3_ref, f_ref, gates_ref, cnt_ref, *, tm, seq):
    i = pl.program_id(1)
    ni = pl.num_programs(1)
    g = ng1_ref[...]
    shift, scale = mod_ref[0:1, :], mod_ref[1:2, :]
    x = x_ref[...]
    a = _rms_mod(x, g, shift, scale)
    a_before = _rms_mod(xp_ref[...], g, shift, scale) * (i > 0).astype(F32)
    a_after = _rms_mod(xn_ref[...], g, shift, scale) * (i < ni - 1).astype(F32)
    ext = jnp.concatenate([a_before, a, a_after], axis=0)
    n_ext = tm + 2 * POOL_HALO

    t = i * tm + lax.broadcasted_iota(jnp.int32, (tm, 1), 0)
    ys = []
    for gi, w in enumerate(POOL_WINDOWS):
        sl = slice(gi * POOL_GROUP_W, (gi + 1) * POOL_GROUP_W)
        e = ext[:, sl]
        acc = e + pltpu.roll(e, 1, 0)
        half = 1
        while 2 * half < w:
            acc = pltpu.roll(acc, n_ext - half, 0) + pltpu.roll(acc, half, 0)
            half *= 2
        win = acc[POOL_HALO:POOL_HALO + tm, :]
        cnt = jnp.minimum(t + w // 2, seq) - jnp.maximum(t - w // 2, 0)
        p = (win / cnt.astype(F32) - a[:, sl]).astype(BF16)
        ys.append(jnp.dot(p, pw_ref[gi], preferred_element_type=F32))
    y = jnp.concatenate(ys, axis=1) * ps_ref[...]

    x3 = x + mod_ref[2:3, :] * y
    x3_ref[...] = x3
    f = _rms_mod(x3, ng2_ref[...], mod_ref[3:4, :], mod_ref[4:5, :])
    f_hi = f.astype(BF16)
    f_ref[...] = f_hi
    f_lo = (f - f_hi.astype(F32)).astype(BF16)
    logits = (jnp.dot(f_hi, rwh_ref[...], preferred_element_type=F32)
              + jnp.dot(f_lo, rwh_ref[...], preferred_element_type=F32)
              + jnp.dot(f_hi, rwl_ref[...], preferred_element_type=F32)) + rb_ref[...]
    lane = lax.broadcasted_iota(jnp.int32, logits.shape, 1)
    neg = jnp.float32(-jnp.inf)
    logits = jnp.where(lane < N_EXPERTS, logits, neg)
    v1 = jnp.max(logits, axis=-1, keepdims=True)
    i1 = jnp.min(jnp.where(logits == v1, lane, LANES), axis=-1, keepdims=True)
    rest = jnp.where(lane == i1, neg, logits)
    v2 = jnp.max(rest, axis=-1, keepdims=True)
    i2 = jnp.min(jnp.where(rest == v2, lane, LANES), axis=-1, keepdims=True)
    e2 = jnp.exp(v2 - v1)
    p1 = 1.0 / (1.0 + e2)
    p2 = e2 / (1.0 + e2)
    gates = jnp.where(lane == i1, p1, jnp.where(lane == i2, p2, 0.0))
    gates_ref[...] = gates
    routed = jnp.where(gates != 0.0, 1.0, 0.0)
    for h in range(tm // DISPATCH_T):
        cnt = jnp.sum(routed[h * DISPATCH_T:(h + 1) * DISPATCH_T, :], axis=0, keepdims=True)
        cnt_ref[h:h + 1, :] = cnt.astype(jnp.int32)


def _pool_router(x2, mod, ng1, ng2, pw, ps, rw, rb, tm):
    b, s, d = x2.shape
    nt = s // tm
    hb = tm // POOL_HALO
    rwh = rw.astype(BF16)
    rwl = (rw - rwh.astype(F32)).astype(BF16)
    tile = lambda w: pl.BlockSpec((None, tm, w), lambda bi, i: (bi, i, 0))
    return pl.pallas_call(
        functools.partial(_pool_kernel, tm=tm, seq=s),
        grid=(b, nt),
        in_specs=[
            tile(d),
            pl.BlockSpec((None, POOL_HALO, d), lambda bi, i: (bi, jnp.maximum(i * hb - 1, 0), 0)),
            pl.BlockSpec((None, POOL_HALO, d),
                         lambda bi, i: (bi, jnp.minimum((i + 1) * hb, nt * hb - 1), 0)),
            pl.BlockSpec((None, SUBLANES, d), lambda bi, i: (bi, 0, 0)),
            _const_spec((1, d)), _const_spec((1, d)), _const_spec(pw.shape), _const_spec((1, d)),
            _const_spec(rw.shape), _const_spec(rw.shape), _const_spec(rb.shape),
        ],
        out_specs=[tile(d), tile(d), tile(LANES),
                   pl.BlockSpec((None, None, tm // DISPATCH_T, LANES), lambda bi, i: (bi, i, 0, 0))],
        out_shape=[jax.ShapeDtypeStruct((b, s, d), F32), jax.ShapeDtypeStruct((b, s, d), BF16),
                   jax.ShapeDtypeStruct((b, s, LANES), F32),
                   jax.ShapeDtypeStruct((b, nt, tm // DISPATCH_T, LANES), jnp.int32)],
        compiler_params=_cparams("parallel", "parallel"),
        name="pool_router",
    )(x2, x2, x2, mod, ng1, ng2, pw, ps, rwh, rwl, rb)


def _seg_copy(vmem_buf, slot, hbm, loc, dst, j, sem, to_hbm):
    v = vmem_buf.at[slot, pl.ds(pl.multiple_of(loc + j * SEG_ALIGN, SEG_ALIGN), SEG_ALIGN)]
    h = hbm.at[pl.ds(pl.multiple_of(dst + j * SEG_ALIGN, SEG_ALIGN), SEG_ALIGN)]
    return pltpu.make_async_copy(v, h, sem) if to_hbm else pltpu.make_async_copy(h, v, sem)


def _for_each_chunk(loc_s, gdst_s, nch_s, tile, fn):
    for e in range(N_EXPERTS):
        k = tile * N_EXPERTS + e
        loc, dst = loc_s[k], gdst_s[k]

        def body(j, carry, loc=loc, dst=dst):
            fn(loc, dst, j)
            return carry

        lax.fori_loop(0, nch_s[k], body, 0)


def _dispatch_kernel(loc_s, gdst_s, nch_s, tail_s, f_ref, gates_ref, locrow_ref, tril_ref,
                     xs_hbm, d_ref, sbuf, zbuf, sem, zsem):
    i = pl.program_id(0)
    n_tiles = pl.num_programs(0)
    slot = i % 2
    t = f_ref.shape[0]
    sel = gates_ref[...] != 0.0
    onehot = jnp.where(sel, 1.0, 0.0).astype(BF16)
    earlier = jnp.dot(tril_ref[...], onehot, preferred_element_type=F32)
    d = jnp.where(sel, earlier + locrow_ref[...], -1.0)
    d_ref[...] = d
    dt = d.T
    row = lax.broadcasted_iota(jnp.int32, (DISPATCH_ROWS, t), 0).astype(F32)
    perm = jnp.zeros((DISPATCH_ROWS, t), F32)
    for e in range(N_EXPERTS):
        perm = jnp.where(row == dt[e:e + 1, :], 1.0, perm)
    sbuf[slot] = jnp.dot(perm.astype(BF16), f_ref[...], preferred_element_type=F32).astype(BF16)

    _for_each_chunk(loc_s, gdst_s, nch_s, i,
                    lambda loc, dst, j: _seg_copy(sbuf, slot, xs_hbm, loc, dst, j, sem.at[slot], True).start())

    def wait_tile(tile, s):
        _for_each_chunk(loc_s, gdst_s, nch_s, tile,
                        lambda loc, dst, j: _seg_copy(sbuf, s, xs_hbm, loc, dst, j, sem.at[s], True).wait())

    @pl.when(i > 0)
    def _():
        wait_tile(i - 1, 1 - slot)

    @pl.when(i == n_tiles - 1)
    def _():
        zbuf[...] = jnp.zeros(zbuf.shape, BF16)
        n_blk = xs_hbm.shape[0] // EXPERT_RB

        def gap_copy(e, j):
            dst = pl.multiple_of(tail_s[e] + j * SEG_ALIGN, SEG_ALIGN)
            return pltpu.make_async_copy(zbuf.at[pl.ds(0, SEG_ALIGN)], xs_hbm.at[pl.ds(dst, SEG_ALIGN)], zsem)

        def blk_copy(blk):
            dst = pl.multiple_of(blk * EXPERT_RB, EXPERT_RB)
            return pltpu.make_async_copy(zbuf, xs_hbm.at[pl.ds(dst, EXPERT_RB)], zsem)

        def fill(issue):
            def run(copy):
                copy.start() if issue else copy.wait()

            for e in range(N_EXPERTS):
                def gap_body(j, carry, e=e):
                    run(gap_copy(e, j))
                    return carry

                lax.fori_loop(0, tail_s[N_EXPERTS + e], gap_body, 0)

            def blk_body(blk, carry):
                run(blk_copy(blk))
                return carry

            lax.fori_loop(tail_s[2 * N_EXPERTS], n_blk, blk_body, 0)

        fill(True)
        wait_tile(i, slot)
        fill(False)


def _dispatch(f, gates, loc, gdst, nch, tail, n_blk):
    n, d = f.shape
    t = DISPATCH_T
    n_tiles = n // t
    locrow = jnp.pad(loc.astype(F32), ((0, 0), (0, LANES - N_EXPERTS)))[:, None, :]
    tril = (jnp.arange(t)[:, None] > jnp.arange(t)[None, :]).astype(BF16)
    grid_spec = pltpu.PrefetchScalarGridSpec(
        num_scalar_prefetch=4,
        grid=(n_tiles,),
        in_specs=[pl.BlockSpec((t, d), lambda i, *_: (i, 0)),
                  pl.BlockSpec((t, LANES), lambda i, *_: (i, 0)),
                  pl.BlockSpec((None, 1, LANES), lambda i, *_: (i, 0, 0)),
                  pl.BlockSpec((t, t), lambda i, *_: (0, 0))],
        out_specs=[pl.BlockSpec(memory_space=pl.ANY),
                   pl.BlockSpec((t, LANES), lambda i, *_: (i, 0))],
        scratch_shapes=[pltpu.VMEM((2, DISPATCH_ROWS, d), BF16),
                        pltpu.VMEM((EXPERT_RB, d), BF16),
                        pltpu.SemaphoreType.DMA((2,)),
                        pltpu.SemaphoreType.DMA(())],
    )
    return pl.pallas_call(
        _dispatch_kernel,
        grid_spec=grid_spec,
        out_shape=[jax.ShapeDtypeStruct((n_blk * EXPERT_RB, d), BF16),
                   jax.ShapeDtypeStruct((n, LANES), F32)],
        compiler_params=_cparams("arbitrary"),
        name="dispatch",
    )(loc.reshape(-1), gdst.reshape(-1), nch.reshape(-1), tail, f, gates, locrow, tril)


def _expert_kernel(exp_s, nvalid_s, xs_ref, wg_ref, wu_ref, wd_ref, ys_ref):
    used = pl.program_id(0) < nvalid_s[0]

    @pl.when(used)
    def _():
        f = xs_ref[...]
        hg = jnp.dot(f, wg_ref[...], preferred_element_type=F32)
        hu = jnp.dot(f, wu_ref[...], preferred_element_type=F32)
        h = (hg * (1.0 / (1.0 + jnp.exp(-hg))) * hu).astype(BF16)
        ys_ref[...] = jnp.dot(h, wd_ref[...], preferred_element_type=F32).astype(BF16)

    @pl.when(jnp.logical_not(used))
    def _():
        ys_ref[...] = jnp.zeros(ys_ref.shape, BF16)


def _experts(xs, blk_exp, n_valid, wg, wu, wd):
    rows, d = xs.shape
    dff = wg.shape[-1]
    grid_spec = pltpu.PrefetchScalarGridSpec(
        num_scalar_prefetch=2,
        grid=(rows // EXPERT_RB,),
        in_specs=[pl.BlockSpec((EXPERT_RB, d), lambda b, exp_s, nv: (b, 0)),
                  pl.BlockSpec((None, d, dff), lambda b, exp_s, nv: (exp_s[b], 0, 0)),
                  pl.BlockSpec((None, d, dff), lambda b, exp_s, nv: (exp_s[b], 0, 0)),
                  pl.BlockSpec((None, dff, d), lambda b, exp_s, nv: (exp_s[b], 0, 0))],
        out_specs=pl.BlockSpec((EXPERT_RB, d), lambda b, exp_s, nv: (b, 0)),
    )
    return pl.pallas_call(
        _expert_kernel,
        grid_spec=grid_spec,
        out_shape=jax.ShapeDtypeStruct((rows, d), BF16),
        compiler_params=_cparams("arbitrary"),
        name="experts",
    )(blk_exp, n_valid, xs, wg, wu, wd)


def _combine_kernel(loc_s, gdst_s, nch_s, x_ref, gates_ref, d_ref, mod_ref, fg_ref, ys_hbm,
                    o_ref, ybuf, sem):
    i = pl.program_id(0)
    n_tiles = pl.num_programs(0)
    slot = i % 2
    t = x_ref.shape[0]

    def fetch(tile, s):
        _for_each_chunk(loc_s, gdst_s, nch_s, tile,
                        lambda loc, dst, j: _seg_copy(ybuf, s, ys_hbm, loc, dst, j, sem.at[s], False).start())

    @pl.when(i == 0)
    def _():
        ybuf[...] = jnp.zeros(ybuf.shape, BF16)
        fetch(i, slot)

    @pl.when(i + 1 < n_tiles)
    def _():
        fetch(i + 1, 1 - slot)

    _for_each_chunk(loc_s, gdst_s, nch_s, i,
                    lambda loc, dst, j: _seg_copy(ybuf, slot, ys_hbm, loc, dst, j, sem.at[slot], False).wait())

    gates, d = gates_ref[...], d_ref[...]
    lane = lax.broadcasted_iota(jnp.int32, (t, DISPATCH_ROWS), 1).astype(F32)
    gmat = jnp.zeros((t, DISPATCH_ROWS), F32)
    for e in range(N_EXPERTS):
        gmat = jnp.where(lane == d[:, e:e + 1], gates[:, e:e + 1], gmat)
    moe = jnp.dot(gmat.astype(BF16), ybuf[slot], preferred_element_type=F32)
    x4 = x_ref[...] + mod_ref[5:6, :] * moe
    ms = jnp.mean(x4 * x4, axis=-1, keepdims=True)
    o_ref[...] = x4 * lax.rsqrt(ms + EPS) * fg_ref[...]


def _combine(x3, gates, dmap, ys, mod, fg, loc, gdst, nch, tiles_per_batch):
    n, d = x3.shape
    t = DISPATCH_T
    grid_spec = pltpu.PrefetchScalarGridSpec(
        num_scalar_prefetch=3,
        grid=(n // t,),
        in_specs=[pl.BlockSpec((t, d), lambda i, *_: (i, 0)),
                  pl.BlockSpec((t, LANES), lambda i, *_: (i, 0)),
                  pl.BlockSpec((t, LANES), lambda i, *_: (i, 0)),
                  pl.BlockSpec((None, SUBLANES, d), lambda i, *_: (i // tiles_per_batch, 0, 0)),
                  pl.BlockSpec((1, d), lambda i, *_: (0, 0)),
                  pl.BlockSpec(memory_space=pl.ANY)],
        out_specs=pl.BlockSpec((t, d), lambda i, *_: (i, 0)),
        scratch_shapes=[pltpu.VMEM((2, DISPATCH_ROWS, d), BF16),
                        pltpu.SemaphoreType.DMA((2,))],
    )
    return pl.pallas_call(
        _combine_kernel,
        grid_spec=grid_spec,
        out_shape=jax.ShapeDtypeStruct((n, d), F32),
        compiler_params=_cparams("arbitrary"),
        name="combine",
    )(loc.reshape(-1), gdst.reshape(-1), nch.reshape(-1), x3, gates, dmap, mod, fg, ys)


def _routing_tables(cnt, n_tokens):
    n_tiles = cnt.shape[0]
    i32 = lambda a: a.astype(jnp.int32)
    npad = (cnt + SEG_ALIGN - 1) // SEG_ALIGN * SEG_ALIGN
    loc = jnp.cumsum(npad, axis=1) - npad
    tot = jnp.sum(npad, axis=0)
    nblk = (tot + EXPERT_RB - 1) // EXPERT_RB
    ends = jnp.cumsum(nblk)
    base = (ends - nblk) * EXPERT_RB
    gdst = base[None, :] + jnp.cumsum(npad, axis=0) - npad
    nch = npad // SEG_ALIGN
    n_blk = (2 * n_tokens + (SEG_ALIGN - 1) * N_EXPERTS * n_tiles) // EXPERT_RB + N_EXPERTS
    blk_exp = jnp.minimum(jnp.searchsorted(ends, jnp.arange(n_blk), side="right"), N_EXPERTS - 1)
    tail = jnp.concatenate([base + tot, (nblk * EXPERT_RB - tot) // SEG_ALIGN, ends[-1:]])
    return n_blk, i32(loc), i32(gdst), i32(nch), i32(tail), i32(blk_exp), i32(ends[-1:])


def _rope_tables(n):
    t = jnp.arange(n)
    pos = jnp.stack([t // GRID_W, t % GRID_W], axis=1).astype(F32)
    freqs = ROPE_BASE ** (-jnp.arange(0, ROPE_AXIS_DIM, 2, dtype=F32) / ROPE_AXIS_DIM)
    ang = pos[:, :, None] * freqs
    ang = jnp.broadcast_to(ang[:, :, None, :], (n, 2, 2, ROPE_AXIS_DIM // 2))
    sign = jnp.array([-1.0, 1.0], F32)[None, None, :, None]
    cos = jnp.cos(ang).reshape(n, HEAD_DIM)
    sin = (jnp.sin(ang) * sign).reshape(n, HEAD_DIM)
    return jnp.tile(cos, (1, LANES // HEAD_DIM)), jnp.tile(sin, (1, LANES // HEAD_DIM))


def _tile_sizes(s):
    return min(512, s), min(512, s), min(2048, s)


def _pad_rows(m, rows):
    return jnp.pad(m, ((0, 0),) * (m.ndim - 2) + ((0, rows - m.shape[-2]), (0, 0)))


def kernel(x, c, ctx, c_ctx, w_mod, b_mod, norm_g, final_norm_g, w_mix_in, q_norm_g, k_norm_g, conv_w, w_mix_out, ffn_w_gate, ffn_w_up, ffn_w_down, pool_w, pool_scale, router_w, router_b, exp_w_gate, exp_w_up, exp_w_down):
    b, s, d = x.shape
    tm, tq, tk = _tile_sizes(s)

    rows = -(-(b + 1) // SUBLANES) * SUBLANES
    cvec = _pad_rows(jnp.concatenate([c, c_ctx[None, :]], axis=0), rows)
    mods = _adaln(cvec, w_mod, b_mod).reshape(w_mod.shape[0], rows, N_MOD, d)
    mod_lat = [_pad_rows(mods[l, :b], SUBLANES) for l in range(2)]
    mod_ctx0 = _pad_rows(mods[0, b], SUBLANES)

    w_in = w_mix_in[0].astype(BF16)
    cuts = (ATTN_W, ATTN_W + KV_W, ATTN_W + 2 * KV_W)
    wq, wk, wv, wc = w_in[:, :cuts[0]], w_in[:, cuts[0]:cuts[1]], w_in[:, cuts[1]:cuts[2]], w_in[:, cuts[2]:]
    head = jnp.arange(ATTN_W) // HEAD_DIM
    bd = ((head[:, None] == head[None, :]).astype(F32) / HEAD_DIM).astype(BF16)
    qg = jnp.tile(q_norm_g[0], N_HEADS)[None, :]
    kg = jnp.tile(k_norm_g[0], N_KV_HEADS)[None, :]
    cw = _pad_rows(conv_w[0], SUBLANES)
    cos, sin = _rope_tables(s)
    ng0a, ng0b = norm_g[0, 0][None, :], norm_g[0, 1][None, :]

    shift = (Q_SCALE * HEAD_DIM * SHIFT_MARGIN) * jnp.max(jnp.abs(q_norm_g[0])) * jnp.max(jnp.abs(k_norm_g[0]))
    qbias = jnp.where(jnp.arange(LANES) == HEAD_DIM, -shift, 0.0).astype(F32)[None, :]
    q, kext, vext, conv = _l0_in(x, mod_lat[0], ng0a, wq, wk, wv, wc, bd, qg, kg, cw, cos, sin, qbias, tm)
    kext_c, vext_c = _ctx_kv(ctx, mod_ctx0, ng0a, wk, wv, bd, kg)
    attn = lax.cond(
        shift <= MAX_FIXED_SHIFT,
        lambda *a: _attention(*a, tq, tk, False),
        lambda *a: _attention(*a, tq, tk, True),
        q, kext_c, vext_c, kext, vext)

    w_out = w_mix_out[0].astype(BF16)
    x1, f1 = _l0_out(x, attn, conv, mod_lat[0], ng0b, w_out[:ATTN_W], w_out[ATTN_W:], tm)
    n_chunks = 2
    dff = ffn_w_gate.shape[-1]
    split_cols = lambda w: w.astype(BF16).reshape(d, n_chunks, dff // n_chunks).transpose(1, 0, 2)
    wd0 = ffn_w_down[0].astype(BF16).reshape(n_chunks, dff // n_chunks, d)
    x2 = _ffn(x1, f1, mod_lat[0], split_cols(ffn_w_gate[0]), split_cols(ffn_w_up[0]), wd0, tm)

    rw = jnp.pad(router_w[0], ((0, 0), (0, LANES - N_EXPERTS)))
    rb = jnp.pad(router_b[0], (0, LANES - N_EXPERTS))[None, :]
    x3, f2, gates, cnt = _pool_router(x2, mod_lat[1], norm_g[1, 0][None, :], norm_g[1, 1][None, :],
                                      pool_w[0].astype(BF16), pool_scale[0][None, :], rw, rb, tm)
    n = b * s
    n_blk, loc, gdst, nch, tail, blk_exp, n_valid = _routing_tables(
        cnt.reshape(n // DISPATCH_T, LANES)[:, :N_EXPERTS], n)
    gates = gates.reshape(n, LANES)
    xs, dmap = _dispatch(f2.reshape(n, d), gates, loc, gdst, nch, tail, n_blk)
    ys = _experts(xs, blk_exp, n_valid,
                  exp_w_gate[0].astype(BF16), exp_w_up[0].astype(BF16), exp_w_down[0].astype(BF16))
    out = _combine(x3.reshape(n, d), gates, dmap, ys, mod_lat[1], final_norm_g[None, :],
                   loc, gdst, nch, s // DISPATCH_T)
    return out.reshape(b, s, d)
```

```python
import functools
import math

import jax
import jax.numpy as jnp
from jax import lax
from jax.experimental import pallas as pl
from jax.experimental.pallas import tpu as pltpu

F32 = jnp.float32
BF16 = jnp.bfloat16

D_MODEL = 1024
GRID_W = 64
N_HEADS = 8
N_KV_HEADS = 2
HEAD_DIM = 64
ATTN_W = N_HEADS * HEAD_DIM
KV_W = N_KV_HEADS * HEAD_DIM
CONV_W = D_MODEL - ATTN_W
ROPE_AXIS_DIM = HEAD_DIM // 2
ROPE_BASE = 10000.0
POOL_WINDOWS = (2, 4, 8, 16)
POOL_GROUP_W = D_MODEL // len(POOL_WINDOWS)
POOL_HALO = 8
N_EXPERTS = 8
DISPATCH_T = 256
SEG_ALIGN = 16
DISPATCH_ROWS = 2 * DISPATCH_T + N_EXPERTS * SEG_ALIGN
EXPERT_RB = 512
N_MOD = 6
EPS = 1e-6

LANES = 128
SUBLANES = 8
VMEM_LIMIT = 56 * 1024 * 1024

Q_SCALE = (HEAD_DIM ** -0.5) * math.log2(math.e)
SHIFT_MARGIN = 1.01
MAX_FIXED_SHIFT = 48.0


def _cparams(*sem):
    return pltpu.CompilerParams(dimension_semantics=sem, vmem_limit_bytes=VMEM_LIMIT)


def _const_spec(shape):
    nd = len(shape)
    return pl.BlockSpec(shape, lambda *_: (0,) * nd, pipeline_mode=pl.Buffered(1))


def _rms_mod(x, g, shift, scale):
    ms = jnp.mean(x * x, axis=-1, keepdims=True)
    return x * lax.rsqrt(ms + EPS) * g * (1.0 + scale) + shift


def _head_rms(z, bd, g):
    sq = z * z
    hi = sq.astype(BF16)
    lo = (sq - hi.astype(F32)).astype(BF16)
    ms = (jnp.dot(hi, bd, preferred_element_type=F32)
          + jnp.dot(lo, bd, preferred_element_type=F32))
    return z * lax.rsqrt(ms + EPS) * g


def _rope128(x, cos, sin_signed):
    lane = lax.broadcasted_iota(jnp.int32, x.shape, 1)
    partner = jnp.where((lane & 16) == 0, pltpu.roll(x, LANES - 16, 1), pltpu.roll(x, 16, 1))
    return x * cos + partner * sin_signed


def _adaln_kernel(c_ref, w_ref, b_ref, o_ref):
    c = c_ref[...]
    s = c * (1.0 / (1.0 + jnp.exp(-c)))
    o_ref[...] = jnp.dot(s, w_ref[...], preferred_element_type=F32,
                         precision=lax.Precision.HIGHEST) + b_ref[...]


def _adaln(cvec, w_mod, b_mod):
    depth, d, n = w_mod.shape
    rows = cvec.shape[0]
    tn = 1536
    return pl.pallas_call(
        _adaln_kernel,
        grid=(depth, n // tn),
        in_specs=[pl.BlockSpec((rows, d), lambda l, j: (0, 0)),
                  pl.BlockSpec((None, d, tn), lambda l, j: (l, 0, j)),
                  pl.BlockSpec((None, 1, tn), lambda l, j: (l, 0, j))],
        out_specs=pl.BlockSpec((None, rows, tn), lambda l, j: (l, 0, j)),
        out_shape=jax.ShapeDtypeStruct((depth, rows, n), F32),
        compiler_params=_cparams("parallel", "parallel"),
        name="adaln",
    )(cvec, w_mod, b_mod.reshape(depth, 1, n))


def _kv_outputs(zk, zv, kext_ref, vext_ref):
    rows = zk.shape[0]
    lane = lax.broadcasted_iota(jnp.int32, zk.shape, 1)
    pad = jnp.where(lane == HEAD_DIM, 1.0, 0.0)
    zk_swapped = pltpu.roll(zk, HEAD_DIM, 1)
    vt = zv.T
    sub = lax.broadcasted_iota(jnp.int32, (LANES - HEAD_DIM, rows), 0)
    ones_row = jnp.where(sub == 0, 1.0, 0.0)
    for g in range(N_KV_HEADS):
        kg = zk if g == 0 else zk_swapped
        kext_ref[g] = jnp.where(lane < HEAD_DIM, kg, pad).astype(BF16)
        vext_ref[g] = jnp.concatenate([vt[g * HEAD_DIM:(g + 1) * HEAD_DIM, :], ones_row], axis=0).astype(BF16)


def _l0_in_kernel(x_ref, xp_ref, xn_ref, mod_ref, ng_ref, wq_ref, wk_ref, wv_ref, wc_ref,
                  bd_ref, qg_ref, kg_ref, cw_ref, cos_ref, sin_ref, qb_ref,
                  q_ref, kext_ref, vext_ref, conv_ref, *, tm):
    i = pl.program_id(1)
    ni = pl.num_programs(1)
    g = ng_ref[...]
    shift, scale = mod_ref[0:1, :], mod_ref[1:2, :]
    a = _rms_mod(x_ref[...], g, shift, scale).astype(BF16)
    cos, sin = cos_ref[...], sin_ref[...]

    zq = jnp.dot(a, wq_ref[...], preferred_element_type=F32)
    qn = _head_rms(zq, bd_ref[...], qg_ref[...])
    qbias = qb_ref[...]
    for c in range(ATTN_W // LANES):
        rt = (_rope128(qn[:, c * LANES:(c + 1) * LANES], cos, sin) * Q_SCALE).T
        for hh in range(2):
            q_ref[2 * c + hh] = jnp.concatenate(
                [rt[hh * HEAD_DIM:(hh + 1) * HEAD_DIM, :], qbias], axis=0).astype(BF16)

    zk = jnp.dot(a, wk_ref[...], preferred_element_type=F32)
    kn = _rope128(_head_rms(zk, bd_ref[0:KV_W, 0:KV_W], kg_ref[...]), cos, sin)
    zv = jnp.dot(a, wv_ref[...], preferred_element_type=F32)
    _kv_outputs(kn, zv, kext_ref, vext_ref)

    zc = jnp.dot(a, wc_ref[...], preferred_element_type=F32)
    gb, w = zc[:, 0:CONV_W], zc[:, CONV_W:2 * CONV_W] * zc[:, 2 * CONV_W:3 * CONV_W]
    halo = jnp.concatenate([xp_ref[...], xn_ref[...]], axis=0)
    ah = _rms_mod(halo, g, shift, scale).astype(BF16)
    zh = jnp.dot(ah, wc_ref[:, CONV_W:3 * CONV_W], preferred_element_type=F32)
    wh = zh[:, 0:CONV_W] * zh[:, CONV_W:2 * CONV_W]
    w_before = wh[SUBLANES - 1:SUBLANES, :] * (i > 0).astype(F32)
    w_after = wh[SUBLANES:SUBLANES + 1, :] * (i < ni - 1).astype(F32)
    row = lax.broadcasted_iota(jnp.int32, w.shape, 0)
    w_prev = jnp.where(row == 0, w_before, pltpu.roll(w, 1, 0))
    w_next = jnp.where(row == tm - 1, w_after, pltpu.roll(w, tm - 1, 0))
    y = gb * (w_prev * cw_ref[0:1, :] + w * cw_ref[1:2, :] + w_next * cw_ref[2:3, :])
    conv_ref[...] = y.astype(BF16)


def _l0_in(x, mod, ng, wq, wk, wv, wc, bd, qg, kg, cw, cos, sin, qbias, tm):
    b, s, d = x.shape
    nt = s // tm
    hb = tm // SUBLANES
    return pl.pallas_call(
        functools.partial(_l0_in_kernel, tm=tm),
        grid=(b, nt),
        in_specs=[
            pl.BlockSpec((None, tm, d), lambda bi, i: (bi, i, 0)),
            pl.BlockSpec((None, SUBLANES, d), lambda bi, i: (bi, jnp.maximum(i * hb - 1, 0), 0)),
            pl.BlockSpec((None, SUBLANES, d),
                         lambda bi, i: (bi, jnp.minimum((i + 1) * hb, nt * hb - 1), 0)),
            pl.BlockSpec((None, SUBLANES, d), lambda bi, i: (bi, 0, 0)),
            _const_spec((1, d)),
            _const_spec(wq.shape), _const_spec(wk.shape), _const_spec(wv.shape),
            _const_spec(wc.shape), _const_spec(bd.shape),
            _const_spec(qg.shape), _const_spec(kg.shape), _const_spec(cw.shape),
            pl.BlockSpec((tm, LANES), lambda bi, i: (i, 0)),
            pl.BlockSpec((tm, LANES), lambda bi, i: (i, 0)),
            _const_spec(qbias.shape),
        ],
        out_specs=[
            pl.BlockSpec((None, N_HEADS, LANES, tm), lambda bi, i: (bi, 0, 0, i)),
            pl.BlockSpec((None, N_KV_HEADS, tm, LANES), lambda bi, i: (bi, 0, i, 0)),
            pl.BlockSpec((None, N_KV_HEADS, LANES, tm), lambda bi, i: (bi, 0, 0, i)),
            pl.BlockSpec((None, tm, CONV_W), lambda bi, i: (bi, i, 0)),
        ],
        out_shape=[
            jax.ShapeDtypeStruct((b, N_HEADS, LANES, s), BF16),
            jax.ShapeDtypeStruct((b, N_KV_HEADS, s, LANES), BF16),
            jax.ShapeDtypeStruct((b, N_KV_HEADS, LANES, s), BF16),
            jax.ShapeDtypeStruct((b, s, CONV_W), BF16),
        ],
        compiler_params=_cparams("parallel", "parallel"),
        name="l0_in",
    )(x, x, x, mod, ng, wq, wk, wv, wc, bd, qg, kg, cw, cos, sin, qbias)


def _ctx_kv_kernel(x_ref, mod_ref, ng_ref, wk_ref, wv_ref, bd_ref, kg_ref, kext_ref, vext_ref):
    a = _rms_mod(x_ref[...], ng_ref[...], mod_ref[0:1, :], mod_ref[1:2, :]).astype(BF16)
    zk = jnp.dot(a, wk_ref[...], preferred_element_type=F32)
    kn = _head_rms(zk, bd_ref[0:KV_W, 0:KV_W], kg_ref[...])
    zv = jnp.dot(a, wv_ref[...], preferred_element_type=F32)
    _kv_outputs(kn, zv, kext_ref, vext_ref)


def _ctx_kv(ctx, mod, ng, wk, wv, bd, kg):
    b, nc, d = ctx.shape
    return pl.pallas_call(
        _ctx_kv_kernel,
        grid=(b,),
        in_specs=[
            pl.BlockSpec((None, nc, d), lambda bi: (bi, 0, 0)),
            _const_spec(mod.shape), _const_spec((1, d)),
            _const_spec(wk.shape), _const_spec(wv.shape), _const_spec(bd.shape),
            _const_spec(kg.shape),
        ],
        out_specs=[
            pl.BlockSpec((None, N_KV_HEADS, nc, LANES), lambda bi: (bi, 0, 0, 0)),
            pl.BlockSpec((None, N_KV_HEADS, LANES, nc), lambda bi: (bi, 0, 0, 0)),
        ],
        out_shape=[
            jax.ShapeDtypeStruct((b, N_KV_HEADS, nc, LANES), BF16),
            jax.ShapeDtypeStruct((b, N_KV_HEADS, LANES, nc), BF16),
        ],
        compiler_params=_cparams("parallel"),
        name="ctx_kv",
    )(ctx, mod, ng, wk, wv, bd, kg)


def _attn_kernel(q_ref, kc_ref, vc_ref, k_ref, v_ref, o_ref, qs_ref, acc_ref, *m_scratch,
                 tq, tk, nk, online):
    n_h = q_ref.shape[0]
    for h in range(n_h):
        qs_ref[:, h * tq:(h + 1) * tq] = q_ref[h]

    if online:
        m_ref, = m_scratch
        m_ref[...] = jnp.full(m_ref.shape, -jnp.inf, F32)
        acc_ref[...] = jnp.zeros(acc_ref.shape, F32)

    def step(kx, vt, first):
        s = jnp.dot(kx, qs_ref[...], preferred_element_type=F32)
        if online:
            m_prev = m_ref[...]
            m_new = jnp.maximum(m_prev, jnp.max(s, axis=0, keepdims=True))
            p = jnp.exp2(s - m_new[0:1, :]).astype(BF16)
            acc_ref[...] = (jnp.exp2(m_prev - m_new)[0:1, :] * acc_ref[...]
                            + jnp.dot(vt, p, preferred_element_type=F32))
            m_ref[...] = m_new
        else:
            pv = jnp.dot(vt, jnp.exp2(s).astype(BF16), preferred_element_type=F32)
            if first:
                acc_ref[...] = pv
            else:
                acc_ref[...] += pv

    step(kc_ref[...], vc_ref[...], True)

    def body(c, carry):
        off = pl.multiple_of(c * tk, tk)
        step(k_ref[pl.ds(off, tk), :], v_ref[:, pl.ds(off, tk)], False)
        return carry

    lax.fori_loop(0, nk, body, 0)

    acc = acc_ref[...]
    r = acc[0:HEAD_DIM, :] / acc[HEAD_DIM:HEAD_DIM + 1, :]
    for p in range(n_h // 2):
        pair = jnp.concatenate([r[:, (2 * p) * tq:(2 * p + 1) * tq],
                                r[:, (2 * p + 1) * tq:(2 * p + 2) * tq]], axis=0)
        o_ref[:, p * LANES:(p + 1) * LANES] = pair.T.astype(BF16)


def _attention(q, kext_c, vext_c, kext, vext, tq, tk, online):
    b, _, _, s = q.shape
    nc = kext_c.shape[-2]
    n_h = N_HEADS // N_KV_HEADS
    cols = n_h * tq
    scratch = [pltpu.VMEM((LANES, cols), BF16), pltpu.VMEM((LANES, cols), F32)]
    if online:
        scratch.append(pltpu.VMEM((SUBLANES, cols), F32))
    return pl.pallas_call(
        functools.partial(_attn_kernel, tq=tq, tk=tk, nk=s // tk, online=online),
        grid=(b, N_KV_HEADS, s // tq),
        in_specs=[
            pl.BlockSpec((None, n_h, LANES, tq), lambda bi, g, i: (bi, g, 0, i)),
            pl.BlockSpec((None, None, nc, LANES), lambda bi, g, i: (bi, g, 0, 0)),
            pl.BlockSpec((None, None, LANES, nc), lambda bi, g, i: (bi, g, 0, 0)),
            pl.BlockSpec((None, None, s, LANES), lambda bi, g, i: (bi, g, 0, 0)),
            pl.BlockSpec((None, None, LANES, s), lambda bi, g, i: (bi, g, 0, 0)),
        ],
        out_specs=pl.BlockSpec((None, tq, n_h * HEAD_DIM), lambda bi, g, i: (bi, i, g)),
        out_shape=jax.ShapeDtypeStruct((b, s, ATTN_W), BF16),
        scratch_shapes=scratch,
        compiler_params=_cparams("parallel", "parallel", "parallel"),
        name="attention_online" if online else "attention",
    )(q, kext_c, vext_c, kext, vext)


def _l0_out_ffn_kernel(x_ref, attn_ref, conv_ref, mod_ref, ng_ref, wa_ref, wc_ref, wg_ref, wu_ref, wd_ref,
                       o_ref, *, n_chunks):
    y = (jnp.dot(attn_ref[...], wa_ref[...], preferred_element_type=F32)
         + jnp.dot(conv_ref[...], wc_ref[...], preferred_element_type=F32))
    x1 = x_ref[...] + mod_ref[2:3, :] * y
    f = _rms_mod(x1, ng_ref[...], mod_ref[3:4, :], mod_ref[4:5, :]).astype(BF16)
    acc = None
    for c in range(n_chunks):
        hg = jnp.dot(f, wg_ref[c], preferred_element_type=F32)
        hu = jnp.dot(f, wu_ref[c], preferred_element_type=F32)
        h = (hg * (1.0 / (1.0 + jnp.exp(-hg))) * hu).astype(BF16)
        part = jnp.dot(h, wd_ref[c], preferred_element_type=F32)
        acc = part if acc is None else acc + part
    o_ref[...] = x1 + mod_ref[5:6, :] * acc


def _l0_out_ffn(x, attn, conv, mod, ng, wa, wc, wg, wu, wd, tm):
    b, s, d = x.shape
    tile = lambda w: pl.BlockSpec((None, tm, w), lambda bi, i: (bi, i, 0))
    return pl.pallas_call(
        functools.partial(_l0_out_ffn_kernel, n_chunks=wg.shape[0]),
        grid=(b, s // tm),
        in_specs=[tile(d), tile(ATTN_W), tile(CONV_W),
                  pl.BlockSpec((None, SUBLANES, d), lambda bi, i: (bi, 0, 0)),
                  _const_spec((1, d)), _const_spec(wa.shape), _const_spec(wc.shape),
                  _const_spec(wg.shape), _const_spec(wu.shape), _const_spec(wd.shape)],
        out_specs=tile(d),
        out_shape=jax.ShapeDtypeStruct((b, s, d), F32),
        compiler_params=_cparams("parallel", "parallel"),
        name="l0_out_ffn",
    )(x, attn, conv, mod, ng, wa, wc, wg, wu, wd)


def _pool_kernel(x_ref, xp_ref, xn_ref, mod_ref, ng1_ref, ng2_ref, pw_ref, ps_ref, rwh_ref, rwl_ref, rb_ref,
                 x3_ref, f_ref, gates_ref, cnt_ref, *, tm, seq):
    i = pl.program_id(1)
    ni = pl.num_programs(1)
    g = ng1_ref[...]
    shift, scale = mod_ref[0:1, :], mod_ref[1:2, :]
    x = x_ref[...]
    a = _rms_mod(x, g, shift, scale)
    a_before = _rms_mod(xp_ref[...], g, shift, scale) * (i > 0).astype(F32)
    a_after = _rms_mod(xn_ref[...], g, shift, scale) * (i < ni - 1).astype(F32)
    ext = jnp.concatenate([a_before, a, a_after], axis=0)
    n_ext = tm + 2 * POOL_HALO

    t = i * tm + lax.broadcasted_iota(jnp.int32, (tm, 1), 0)
    ys = []
    for gi, w in enumerate(POOL_WINDOWS):
        sl = slice(gi * POOL_GROUP_W, (gi + 1) * POOL_GROUP_W)
        e = ext[:, sl]
        acc = e + pltpu.roll(e, 1, 0)
        half = 1
        while 2 * half < w:
            acc = pltpu.roll(acc, n_ext - half, 0) + pltpu.roll(acc, half, 0)
            half *= 2
        win = acc[POOL_HALO:POOL_HALO + tm, :]
        cnt = jnp.minimum(t + w // 2, seq) - jnp.maximum(t - w // 2, 0)
        p = (win / cnt.astype(F32) - a[:, sl]).astype(BF16)
        ys.append(jnp.dot(p, pw_ref[gi], preferred_element_type=F32))
    y = jnp.concatenate(ys, axis=1) * ps_ref[...]

    x3 = x + mod_ref[2:3, :] * y
    x3_ref[...] = x3
    f = _rms_mod(x3, ng2_ref[...], mod_ref[3:4, :], mod_ref[4:5, :])
    f_hi = f.astype(BF16)
    f_ref[...] = f_hi
    f_lo = (f - f_hi.astype(F32)).astype(BF16)
    logits = (jnp.dot(f_hi, rwh_ref[...], preferred_element_type=F32)
              + jnp.dot(f_lo, rwh_ref[...], preferred_element_type=F32)
              + jnp.dot(f_hi, rwl_ref[...], preferred_element_type=F32)) + rb_ref[...]
    lane = lax.broadcasted_iota(jnp.int32, logits.shape, 1)
    neg = jnp.float32(-jnp.inf)
    logits = jnp.where(lane < N_EXPERTS, logits, neg)
    v1 = jnp.max(logits, axis=-1, keepdims=True)
    i1 = jnp.min(jnp.where(logits == v1, lane, LANES), axis=-1, keepdims=True)
    rest = jnp.where(lane == i1, neg, logits)
    v2 = jnp.max(rest, axis=-1, keepdims=True)
    i2 = jnp.min(jnp.where(rest == v2, lane, LANES), axis=-1, keepdims=True)
    e2 = jnp.exp(v2 - v1)
    p1 = 1.0 / (1.0 + e2)
    p2 = e2 / (1.0 + e2)
    gates = jnp.where(lane == i1, p1, jnp.where(lane == i2, p2, 0.0))
    gates_ref[...] = gates
    routed = jnp.where(gates != 0.0, 1.0, 0.0)
    for h in range(tm // DISPATCH_T):
        cnt = jnp.sum(routed[h * DISPATCH_T:(h + 1) * DISPATCH_T, :], axis=0, keepdims=True)
        cnt_ref[h:h + 1, :] = cnt.astype(jnp.int32)


def _pool_router(x2, mod, ng1, ng2, pw, ps, rw, rb, tm):
    b, s, d = x2.shape
    nt = s // tm
    hb = tm // POOL_HALO
    rwh = rw.astype(BF16)
    rwl = (rw - rwh.astype(F32)).astype(BF16)
    tile = lambda w: pl.BlockSpec((None, tm, w), lambda bi, i: (bi, i, 0))
    return pl.pallas_call(
        functools.partial(_pool_kernel, tm=tm, seq=s),
        grid=(b, nt),
        in_specs=[
            tile(d),
            pl.BlockSpec((None, POOL_HALO, d), lambda bi, i: (bi, jnp.maximum(i * hb - 1, 0), 0)),
            pl.BlockSpec((None, POOL_HALO, d),
                         lambda bi, i: (bi, jnp.minimum((i + 1) * hb, nt * hb - 1), 0)),
            pl.BlockSpec((None, SUBLANES, d), lambda bi, i: (bi, 0, 0)),
            _const_spec((1, d)), _const_spec((1, d)), _const_spec(pw.shape), _const_spec((1, d)),
            _const_spec(rw.shape), _const_spec(rw.shape), _const_spec(rb.shape),
        ],
        out_specs=[tile(d), tile(d), tile(LANES),
                   pl.BlockSpec((None, None, tm // DISPATCH_T, LANES), lambda bi, i: (bi, i, 0, 0))],
        out_shape=[jax.ShapeDtypeStruct((b, s, d), F32), jax.ShapeDtypeStruct((b, s, d), BF16),
                   jax.ShapeDtypeStruct((b, s, LANES), F32),
                   jax.ShapeDtypeStruct((b, nt, tm // DISPATCH_T, LANES), jnp.int32)],
        compiler_params=_cparams("parallel", "parallel"),
        name="pool_router",
    )(x2, x2, x2, mod, ng1, ng2, pw, ps, rwh, rwl, rb)


def _seg_copy(vmem_buf, slot, hbm, loc, dst, j, sem, to_hbm):
    v = vmem_buf.at[slot, pl.ds(pl.multiple_of(loc + j * SEG_ALIGN, SEG_ALIGN), SEG_ALIGN)]
    h = hbm.at[pl.ds(pl.multiple_of(dst + j * SEG_ALIGN, SEG_ALIGN), SEG_ALIGN)]
    return pltpu.make_async_copy(v, h, sem) if to_hbm else pltpu.make_async_copy(h, v, sem)


def _for_each_chunk(loc_s, gdst_s, nch_s, tile, fn):
    for e in range(N_EXPERTS):
        k = tile * N_EXPERTS + e
        loc, dst = loc_s[k], gdst_s[k]

        def body(j, carry, loc=loc, dst=dst):
            fn(loc, dst, j)
            return carry

        lax.fori_loop(0, nch_s[k], body, 0)


def _dispatch_kernel(loc_s, gdst_s, nch_s, tail_s, f_ref, gates_ref, locrow_ref, tril_ref,
                     xs_hbm, d_ref, sbuf, zbuf, sem, zsem):
    i = pl.program_id(0)
    n_tiles = pl.num_programs(0)
    slot = i % 2
    t = f_ref.shape[0]
    sel = gates_ref[...] != 0.0
    onehot = jnp.where(sel, 1.0, 0.0).astype(BF16)
    earlier = jnp.dot(tril_ref[...], onehot, preferred_element_type=F32)
    d = jnp.where(sel, earlier + locrow_ref[...], -1.0)
    d_ref[...] = d
    dt = d.T
    row = lax.broadcasted_iota(jnp.int32, (DISPATCH_ROWS, t), 0).astype(F32)
    perm = jnp.zeros((DISPATCH_ROWS, t), F32)
    for e in range(N_EXPERTS):
        perm = jnp.where(row == dt[e:e + 1, :], 1.0, perm)
    sbuf[slot] = jnp.dot(perm.astype(BF16), f_ref[...], preferred_element_type=F32).astype(BF16)

    _for_each_chunk(loc_s, gdst_s, nch_s, i,
                    lambda loc, dst, j: _seg_copy(sbuf, slot, xs_hbm, loc, dst, j, sem.at[slot], True).start())

    def wait_tile(tile, s):
        _for_each_chunk(loc_s, gdst_s, nch_s, tile,
                        lambda loc, dst, j: _seg_copy(sbuf, s, xs_hbm, loc, dst, j, sem.at[s], True).wait())

    @pl.when(i > 0)
    def _():
        wait_tile(i - 1, 1 - slot)

    @pl.when(i == n_tiles - 1)
    def _():
        zbuf[...] = jnp.zeros(zbuf.shape, BF16)
        n_blk = xs_hbm.shape[0] // EXPERT_RB

        def gap_copy(e, j):
            dst = pl.multiple_of(tail_s[e] + j * SEG_ALIGN, SEG_ALIGN)
            return pltpu.make_async_copy(zbuf.at[pl.ds(0, SEG_ALIGN)], xs_hbm.at[pl.ds(dst, SEG_ALIGN)], zsem)

        def blk_copy(blk):
            dst = pl.multiple_of(blk * EXPERT_RB, EXPERT_RB)
            return pltpu.make_async_copy(zbuf, xs_hbm.at[pl.ds(dst, EXPERT_RB)], zsem)

        def fill(issue):
            def run(copy):
                copy.start() if issue else copy.wait()

            for e in range(N_EXPERTS):
                def gap_body(j, carry, e=e):
                    run(gap_copy(e, j))
                    return carry

                lax.fori_loop(0, tail_s[N_EXPERTS + e], gap_body, 0)

            def blk_body(blk, carry):
                run(blk_copy(blk))
                return carry

            lax.fori_loop(tail_s[2 * N_EXPERTS], n_blk, blk_body, 0)

        fill(True)
        wait_tile(i, slot)
        fill(False)


def _dispatch(f, gates, loc, gdst, nch, tail, n_blk):
    n, d = f.shape
    t = DISPATCH_T
    n_tiles = n // t
    locrow = jnp.pad(loc.astype(F32), ((0, 0), (0, LANES - N_EXPERTS)))[:, None, :]
    tril = (jnp.arange(t)[:, None] > jnp.arange(t)[None, :]).astype(BF16)
    grid_spec = pltpu.PrefetchScalarGridSpec(
        num_scalar_prefetch=4,
        grid=(n_tiles,),
        in_specs=[pl.BlockSpec((t, d), lambda i, *_: (i, 0)),
                  pl.BlockSpec((t, LANES), lambda i, *_: (i, 0)),
                  pl.BlockSpec((None, 1, LANES), lambda i, *_: (i, 0, 0)),
                  pl.BlockSpec((t, t), lambda i, *_: (0, 0))],
        out_specs=[pl.BlockSpec(memory_space=pl.ANY),
                   pl.BlockSpec((t, LANES), lambda i, *_: (i, 0))],
        scratch_shapes=[pltpu.VMEM((2, DISPATCH_ROWS, d), BF16),
                        pltpu.VMEM((EXPERT_RB, d), BF16),
                        pltpu.SemaphoreType.DMA((2,)),
                        pltpu.SemaphoreType.DMA(())],
    )
    return pl.pallas_call(
        _dispatch_kernel,
        grid_spec=grid_spec,
        out_shape=[jax.ShapeDtypeStruct((n_blk * EXPERT_RB, d), BF16),
                   jax.ShapeDtypeStruct((n, LANES), F32)],
        compiler_params=_cparams("arbitrary"),
        name="dispatch",
    )(loc.reshape(-1), gdst.reshape(-1), nch.reshape(-1), tail, f, gates, locrow, tril)


def _expert_kernel(exp_s, nvalid_s, xs_ref, wg_ref, wu_ref, wd_ref, ys_ref):
    used = pl.program_id(0) < nvalid_s[0]

    @pl.when(used)
    def _():
        f = xs_ref[...]
        hg = jnp.dot(f, wg_ref[...], preferred_element_type=F32)
        hu = jnp.dot(f, wu_ref[...], preferred_element_type=F32)
        h = (hg * (1.0 / (1.0 + jnp.exp(-hg))) * hu).astype(BF16)
        ys_ref[...] = jnp.dot(h, wd_ref[...], preferred_element_type=F32).astype(BF16)

    @pl.when(jnp.logical_not(used))
    def _():
        ys_ref[...] = jnp.zeros(ys_ref.shape, BF16)


def _experts(xs, blk_exp, n_valid, wg, wu, wd):
    rows, d = xs.shape
    dff = wg.shape[-1]
    grid_spec = pltpu.PrefetchScalarGridSpec(
        num_scalar_prefetch=2,
        grid=(rows // EXPERT_RB,),
        in_specs=[pl.BlockSpec((EXPERT_RB, d), lambda b, exp_s, nv: (b, 0)),
                  pl.BlockSpec((None, d, dff), lambda b, exp_s, nv: (exp_s[b], 0, 0)),
                  pl.BlockSpec((None, d, dff), lambda b, exp_s, nv: (exp_s[b], 0, 0)),
                  pl.BlockSpec((None, dff, d), lambda b, exp_s, nv: (exp_s[b], 0, 0))],
        out_specs=pl.BlockSpec((EXPERT_RB, d), lambda b, exp_s, nv: (b, 0)),
    )
    return pl.pallas_call(
        _expert_kernel,
        grid_spec=grid_spec,
        out_shape=jax.ShapeDtypeStruct((rows, d), BF16),
        compiler_params=_cparams("arbitrary"),
        name="experts",
    )(blk_exp, n_valid, xs, wg, wu, wd)


def _combine_kernel(loc_s, gdst_s, nch_s, x_ref, gates_ref, d_ref, mod_ref, fg_ref, ys_hbm,
                    o_ref, ybuf, sem):
    i = pl.program_id(0)
    n_tiles = pl.num_programs(0)
    slot = i % 2
    t = x_ref.shape[0]

    def fetch(tile, s):
        _for_each_chunk(loc_s, gdst_s, nch_s, tile,
                        lambda loc, dst, j: _seg_copy(ybuf, s, ys_hbm, loc, dst, j, sem.at[s], False).start())

    @pl.when(i == 0)
    def _():
        ybuf[...] = jnp.zeros(ybuf.shape, BF16)
        fetch(i, slot)

    @pl.when(i + 1 < n_tiles)
    def _():
        fetch(i + 1, 1 - slot)

    _for_each_chunk(loc_s, gdst_s, nch_s, i,
                    lambda loc, dst, j: _seg_copy(ybuf, slot, ys_hbm, loc, dst, j, sem.at[slot], False).wait())

    gates, d = gates_ref[...], d_ref[...]
    lane = lax.broadcasted_iota(jnp.int32, (t, DISPATCH_ROWS), 1).astype(F32)
    gmat = jnp.zeros((t, DISPATCH_ROWS), F32)
    for e in range(N_EXPERTS):
        gmat = jnp.where(lane == d[:, e:e + 1], gates[:, e:e + 1], gmat)
    moe = jnp.dot(gmat.astype(BF16), ybuf[slot], preferred_element_type=F32)
    x4 = x_ref[...] + mod_ref[5:6, :] * moe
    ms = jnp.mean(x4 * x4, axis=-1, keepdims=True)
    o_ref[...] = x4 * lax.rsqrt(ms + EPS) * fg_ref[...]


def _combine(x3, gates, dmap, ys, mod, fg, loc, gdst, nch, tiles_per_batch):
    n, d = x3.shape
    t = DISPATCH_T
    grid_spec = pltpu.PrefetchScalarGridSpec(
        num_scalar_prefetch=3,
        grid=(n // t,),
        in_specs=[pl.BlockSpec((t, d), lambda i, *_: (i, 0)),
                  pl.BlockSpec((t, LANES), lambda i, *_: (i, 0)),
                  pl.BlockSpec((t, LANES), lambda i, *_: (i, 0)),
                  pl.BlockSpec((None, SUBLANES, d), lambda i, *_: (i // tiles_per_batch, 0, 0)),
                  pl.BlockSpec((1, d), lambda i, *_: (0, 0)),
                  pl.BlockSpec(memory_space=pl.ANY)],
        out_specs=pl.BlockSpec((t, d), lambda i, *_: (i, 0)),
        scratch_shapes=[pltpu.VMEM((2, DISPATCH_ROWS, d), BF16),
                        pltpu.SemaphoreType.DMA((2,))],
    )
    return pl.pallas_call(
        _combine_kernel,
        grid_spec=grid_spec,
        out_shape=jax.ShapeDtypeStruct((n, d), F32),
        compiler_params=_cparams("arbitrary"),
        name="combine",
    )(loc.reshape(-1), gdst.reshape(-1), nch.reshape(-1), x3, gates, dmap, mod, fg, ys)


def _routing_tables(cnt, n_tokens):
    n_tiles = cnt.shape[0]
    i32 = lambda a: a.astype(jnp.int32)
    npad = (cnt + SEG_ALIGN - 1) // SEG_ALIGN * SEG_ALIGN
    loc = jnp.cumsum(npad, axis=1) - npad
    tot = jnp.sum(npad, axis=0)
    nblk = (tot + EXPERT_RB - 1) // EXPERT_RB
    ends = jnp.cumsum(nblk)
    base = (ends - nblk) * EXPERT_RB
    gdst = base[None, :] + jnp.cumsum(npad, axis=0) - npad
    nch = npad // SEG_ALIGN
    n_blk = (2 * n_tokens + (SEG_ALIGN - 1) * N_EXPERTS * n_tiles) // EXPERT_RB + N_EXPERTS
    blk_exp = jnp.minimum(jnp.searchsorted(ends, jnp.arange(n_blk), side="right"), N_EXPERTS - 1)
    tail = jnp.concatenate([base + tot, (nblk * EXPERT_RB - tot) // SEG_ALIGN, ends[-1:]])
    return n_blk, i32(loc), i32(gdst), i32(nch), i32(tail), i32(blk_exp), i32(ends[-1:])


def _rope_tables(n):
    t = jnp.arange(n)
    pos = jnp.stack([t // GRID_W, t % GRID_W], axis=1).astype(F32)
    freqs = ROPE_BASE ** (-jnp.arange(0, ROPE_AXIS_DIM, 2, dtype=F32) / ROPE_AXIS_DIM)
    ang = pos[:, :, None] * freqs
    ang = jnp.broadcast_to(ang[:, :, None, :], (n, 2, 2, ROPE_AXIS_DIM // 2))
    sign = jnp.array([-1.0, 1.0], F32)[None, None, :, None]
    cos = jnp.cos(ang).reshape(n, HEAD_DIM)
    sin = (jnp.sin(ang) * sign).reshape(n, HEAD_DIM)
    return jnp.tile(cos, (1, LANES // HEAD_DIM)), jnp.tile(sin, (1, LANES // HEAD_DIM))


def _tile_sizes(s):
    return min(512, s), min(512, s), min(2048, s)


def _pad_rows(m, rows):
    return jnp.pad(m, ((0, 0),) * (m.ndim - 2) + ((0, rows - m.shape[-2]), (0, 0)))


def kernel(x, c, ctx, c_ctx, w_mod, b_mod, norm_g, final_norm_g, w_mix_in, q_norm_g, k_norm_g, conv_w, w_mix_out, ffn_w_gate, ffn_w_up, ffn_w_down, pool_w, pool_scale, router_w, router_b, exp_w_gate, exp_w_up, exp_w_down):
    b, s, d = x.shape
    tm, tq, tk = _tile_sizes(s)

    rows = -(-(b + 1) // SUBLANES) * SUBLANES
    cvec = _pad_rows(jnp.concatenate([c, c_ctx[None, :]], axis=0), rows)
    mods = _adaln(cvec, w_mod, b_mod).reshape(w_mod.shape[0], rows, N_MOD, d)
    mod_lat = [_pad_rows(mods[l, :b], SUBLANES) for l in range(2)]
    mod_ctx0 = _pad_rows(mods[0, b], SUBLANES)

    w_in = w_mix_in[0].astype(BF16)
    cuts = (ATTN_W, ATTN_W + KV_W, ATTN_W + 2 * KV_W)
    wq, wk, wv, wc = w_in[:, :cuts[0]], w_in[:, cuts[0]:cuts[1]], w_in[:, cuts[1]:cuts[2]], w_in[:, cuts[2]:]
    head = jnp.arange(ATTN_W) // HEAD_DIM
    bd = ((head[:, None] == head[None, :]).astype(F32) / HEAD_DIM).astype(BF16)
    qg = jnp.tile(q_norm_g[0], N_HEADS)[None, :]
    kg = jnp.tile(k_norm_g[0], N_KV_HEADS)[None, :]
    cw = _pad_rows(conv_w[0], SUBLANES)
    cos, sin = _rope_tables(s)
    ng0a, ng0b = norm_g[0, 0][None, :], norm_g[0, 1][None, :]

    shift = (Q_SCALE * HEAD_DIM * SHIFT_MARGIN) * jnp.max(jnp.abs(q_norm_g[0])) * jnp.max(jnp.abs(k_norm_g[0]))
    qbias = jnp.where(jnp.arange(LANES - HEAD_DIM)[:, None] == 0, -shift, 0.0).astype(F32) * jnp.ones((1, tm), F32)
    q, kext, vext, conv = _l0_in(x, mod_lat[0], ng0a, wq, wk, wv, wc, bd, qg, kg, cw, cos, sin, qbias, tm)
    kext_c, vext_c = _ctx_kv(ctx, mod_ctx0, ng0a, wk, wv, bd, kg)
    attn = lax.cond(
        shift <= MAX_FIXED_SHIFT,
        lambda *a: _attention(*a, tq, tk, False),
        lambda *a: _attention(*a, tq, tk, True),
        q, kext_c, vext_c, kext, vext)

    w_out = w_mix_out[0].astype(BF16)
    n_chunks = 2
    dff = ffn_w_gate.shape[-1]
    split_cols = lambda w: w.astype(BF16).reshape(d, n_chunks, dff // n_chunks).transpose(1, 0, 2)
    wd0 = ffn_w_down[0].astype(BF16).reshape(n_chunks, dff // n_chunks, d)
    x2 = _l0_out_ffn(x, attn, conv, mod_lat[0], ng0b, w_out[:ATTN_W], w_out[ATTN_W:],
                     split_cols(ffn_w_gate[0]), split_cols(ffn_w_up[0]), wd0, tm)

    rw = jnp.pad(router_w[0], ((0, 0), (0, LANES - N_EXPERTS)))
    rb = jnp.pad(router_b[0], (0, LANES - N_EXPERTS))[None, :]
    x3, f2, gates, cnt = _pool_router(x2, mod_lat[1], norm_g[1, 0][None, :], norm_g[1, 1][None, :],
                                      pool_w[0].astype(BF16), pool_scale[0][None, :], rw, rb, tm)
    n = b * s
    n_blk, loc, gdst, nch, tail, blk_exp, n_valid = _routing_tables(
        cnt.reshape(n // DISPATCH_T, LANES)[:, :N_EXPERTS], n)
    gates = gates.reshape(n, LANES)
    xs, dmap = _dispatch(f2.reshape(n, d), gates, loc, gdst, nch, tail, n_blk)
    ys = _experts(xs, blk_exp, n_valid,
                  exp_w_gate[0].astype(BF16), exp_w_up[0].astype(BF16), exp_w_down[0].astype(BF16))
    out = _combine(x3.reshape(n, d), gates, dmap, ys, mod_lat[1], final_norm_g[None, :],
                   loc, gdst, nch, s // DISPATCH_T)
    return out.reshape(b, s, d)
```

```python
import functools
import math

import jax
import jax.numpy as jnp
from jax import lax
from jax.experimental import pallas as pl
from jax.experimental.pallas import tpu as pltpu

F32 = jnp.float32
BF16 = jnp.bfloat16

D_MODEL = 1024
GRID_W = 64
N_HEADS = 8
N_KV_HEADS = 2
HEAD_DIM = 64
ATTN_W = N_HEADS * HEAD_DIM
KV_W = N_KV_HEADS * HEAD_DIM
CONV_W = D_MODEL - ATTN_W
ROPE_AXIS_DIM = HEAD_DIM // 2
ROPE_BASE = 10000.0
POOL_WINDOWS = (2, 4, 8, 16)
POOL_GROUP_W = D_MODEL // len(POOL_WINDOWS)
POOL_HALO = 8
N_EXPERTS = 8
DISPATCH_T = 256
SEG_ALIGN = 16
DISPATCH_ROWS = 2 * DISPATCH_T + N_EXPERTS * SEG_ALIGN
EXPERT_RB = 512
N_MOD = 6
EPS = 1e-6

LANES = 128
SUBLANES = 8
VMEM_LIMIT = 56 * 1024 * 1024

Q_SCALE = (HEAD_DIM ** -0.5) * math.log2(math.e)
SHIFT_MARGIN = 1.01
MAX_FIXED_SHIFT = 48.0


def _cparams(*sem):
    return pltpu.CompilerParams(dimension_semantics=sem, vmem_limit_bytes=VMEM_LIMIT)


def _const_spec(shape):
    nd = len(shape)
    return pl.BlockSpec(shape, lambda *_: (0,) * nd, pipeline_mode=pl.Buffered(1))


def _rms_mod(x, g, shift, scale):
    ms = jnp.mean(x * x, axis=-1, keepdims=True)
    return x * lax.rsqrt(ms + EPS) * g * (1.0 + scale) + shift


def _head_rms(z, bd, g):
    ms = jnp.dot((z * z).astype(BF16), bd, preferred_element_type=F32)
    return z * lax.rsqrt(ms + EPS) * g


def _rope128(x, cos, sin_signed):
    lane = lax.broadcasted_iota(jnp.int32, x.shape, 1)
    partner = jnp.where((lane & 16) == 0, pltpu.roll(x, LANES - 16, 1), pltpu.roll(x, 16, 1))
    return x * cos + partner * sin_signed


def _adaln_kernel(c_ref, w_ref, b_ref, o_ref):
    c = c_ref[...]
    s = c * (1.0 / (1.0 + jnp.exp(-c)))
    o_ref[...] = jnp.dot(s, w_ref[...], preferred_element_type=F32,
                         precision=lax.Precision.HIGHEST) + b_ref[...]


def _adaln(cvec, w_mod, b_mod):
    depth, d, n = w_mod.shape
    rows = cvec.shape[0]
    tn = 1536
    return pl.pallas_call(
        _adaln_kernel,
        grid=(depth, n // tn),
        in_specs=[pl.BlockSpec((rows, d), lambda l, j: (0, 0)),
                  pl.BlockSpec((None, d, tn), lambda l, j: (l, 0, j)),
                  pl.BlockSpec((None, 1, tn), lambda l, j: (l, 0, j))],
        out_specs=pl.BlockSpec((None, rows, tn), lambda l, j: (l, 0, j)),
        out_shape=jax.ShapeDtypeStruct((depth, rows, n), F32),
        compiler_params=_cparams("parallel", "parallel"),
        name="adaln",
    )(cvec, w_mod, b_mod.reshape(depth, 1, n))


def _kv_outputs(zk, zv, kext_ref, vext_ref):
    rows = zk.shape[0]
    lane = lax.broadcasted_iota(jnp.int32, zk.shape, 1)
    pad = jnp.where(lane == HEAD_DIM, 1.0, 0.0)
    zk_swapped = pltpu.roll(zk, HEAD_DIM, 1)
    vt = zv.T
    sub = lax.broadcasted_iota(jnp.int32, (LANES - HEAD_DIM, rows), 0)
    ones_row = jnp.where(sub == 0, 1.0, 0.0)
    for g in range(N_KV_HEADS):
        kg = zk if g == 0 else zk_swapped
        kext_ref[g] = jnp.where(lane < HEAD_DIM, kg, pad).astype(BF16)
        vext_ref[g] = jnp.concatenate([vt[g * HEAD_DIM:(g + 1) * HEAD_DIM, :], ones_row], axis=0).astype(BF16)


def _l0_in_kernel(x_ref, xp_ref, xn_ref, mod_ref, ng_ref, wq_ref, wkv_ref, wc_ref,
                  bd_ref, qg_ref, kg_ref, cw_ref, cos_ref, sin_ref, qb_ref,
                  q_ref, kext_ref, vext_ref, conv_ref, *, tm):
    i = pl.program_id(1)
    ni = pl.num_programs(1)
    g = ng_ref[...]
    shift, scale = mod_ref[0:1, :], mod_ref[1:2, :]
    a = _rms_mod(x_ref[...], g, shift, scale).astype(BF16)
    cos, sin = cos_ref[...], sin_ref[...]

    zq = jnp.dot(a, wq_ref[...], preferred_element_type=F32)
    qn = _head_rms(zq, bd_ref[...], qg_ref[...])
    qbias = qb_ref[...]
    for c in range(ATTN_W // LANES):
        rt = (_rope128(qn[:, c * LANES:(c + 1) * LANES], cos, sin) * Q_SCALE).T
        for hh in range(2):
            q_ref[2 * c + hh] = jnp.concatenate(
                [rt[hh * HEAD_DIM:(hh + 1) * HEAD_DIM, :], qbias], axis=0).astype(BF16)

    zkv = jnp.dot(a, wkv_ref[...], preferred_element_type=F32)
    kn = _rope128(_head_rms(zkv[:, 0:KV_W], bd_ref[0:KV_W, 0:KV_W], kg_ref[...]), cos, sin)
    _kv_outputs(kn, zkv[:, KV_W:2 * KV_W], kext_ref, vext_ref)

    zc = jnp.dot(a, wc_ref[...], preferred_element_type=F32)
    gb, w = zc[:, 0:CONV_W], zc[:, CONV_W:2 * CONV_W] * zc[:, 2 * CONV_W:3 * CONV_W]
    halo = jnp.concatenate([xp_ref[...], xn_ref[...]], axis=0)
    ah = _rms_mod(halo, g, shift, scale).astype(BF16)
    zh = jnp.dot(ah, wc_ref[:, CONV_W:3 * CONV_W], preferred_element_type=F32)
    wh = zh[:, 0:CONV_W] * zh[:, CONV_W:2 * CONV_W]
    w_before = wh[SUBLANES - 1:SUBLANES, :] * (i > 0).astype(F32)
    w_after = wh[SUBLANES:SUBLANES + 1, :] * (i < ni - 1).astype(F32)
    row = lax.broadcasted_iota(jnp.int32, w.shape, 0)
    w_prev = jnp.where(row == 0, w_before, pltpu.roll(w, 1, 0))
    w_next = jnp.where(row == tm - 1, w_after, pltpu.roll(w, tm - 1, 0))
    y = gb * (w_prev * cw_ref[0:1, :] + w * cw_ref[1:2, :] + w_next * cw_ref[2:3, :])
    conv_ref[...] = y.astype(BF16)


def _l0_in(x, mod, ng, wq, wkv, wc, bd, qg, kg, cw, cos, sin, qbias, tm):
    b, s, d = x.shape
    nt = s // tm
    hb = tm // SUBLANES
    return pl.pallas_call(
        functools.partial(_l0_in_kernel, tm=tm),
        grid=(b, nt),
        in_specs=[
            pl.BlockSpec((None, tm, d), lambda bi, i: (bi, i, 0)),
            pl.BlockSpec((None, SUBLANES, d), lambda bi, i: (bi, jnp.maximum(i * hb - 1, 0), 0)),
            pl.BlockSpec((None, SUBLANES, d),
                         lambda bi, i: (bi, jnp.minimum((i + 1) * hb, nt * hb - 1), 0)),
            pl.BlockSpec((None, SUBLANES, d), lambda bi, i: (bi, 0, 0)),
            _const_spec((1, d)),
            _const_spec(wq.shape), _const_spec(wkv.shape),
            _const_spec(wc.shape), _const_spec(bd.shape),
            _const_spec(qg.shape), _const_spec(kg.shape), _const_spec(cw.shape),
            pl.BlockSpec((tm, LANES), lambda bi, i: (i, 0)),
            pl.BlockSpec((tm, LANES), lambda bi, i: (i, 0)),
            _const_spec(qbias.shape),
        ],
        out_specs=[
            pl.BlockSpec((None, N_HEADS, LANES, tm), lambda bi, i: (bi, 0, 0, i)),
            pl.BlockSpec((None, N_KV_HEADS, tm, LANES), lambda bi, i: (bi, 0, i, 0)),
            pl.BlockSpec((None, N_KV_HEADS, LANES, tm), lambda bi, i: (bi, 0, 0, i)),
            pl.BlockSpec((None, tm, CONV_W), lambda bi, i: (bi, i, 0)),
        ],
        out_shape=[
            jax.ShapeDtypeStruct((b, N_HEADS, LANES, s), BF16),
            jax.ShapeDtypeStruct((b, N_KV_HEADS, s, LANES), BF16),
            jax.ShapeDtypeStruct((b, N_KV_HEADS, LANES, s), BF16),
            jax.ShapeDtypeStruct((b, s, CONV_W), BF16),
        ],
        compiler_params=_cparams("parallel", "parallel"),
        name="l0_in",
    )(x, x, x, mod, ng, wq, wkv, wc, bd, qg, kg, cw, cos, sin, qbias)


def _ctx_kv_kernel(x_ref, mod_ref, ng_ref, wkv_ref, bd_ref, kg_ref, kext_ref, vext_ref):
    a = _rms_mod(x_ref[...], ng_ref[...], mod_ref[0:1, :], mod_ref[1:2, :]).astype(BF16)
    zkv = jnp.dot(a, wkv_ref[...], preferred_element_type=F32)
    kn = _head_rms(zkv[:, 0:KV_W], bd_ref[0:KV_W, 0:KV_W], kg_ref[...])
    _kv_outputs(kn, zkv[:, KV_W:2 * KV_W], kext_ref, vext_ref)


def _ctx_kv(ctx, mod, ng, wkv, bd, kg):
    b, nc, d = ctx.shape
    return pl.pallas_call(
        _ctx_kv_kernel,
        grid=(b,),
        in_specs=[
            pl.BlockSpec((None, nc, d), lambda bi: (bi, 0, 0)),
            _const_spec(mod.shape), _const_spec((1, d)),
            _const_spec(wkv.shape), _const_spec(bd.shape), _const_spec(kg.shape),
        ],
        out_specs=[
            pl.BlockSpec((None, N_KV_HEADS, nc, LANES), lambda bi: (bi, 0, 0, 0)),
            pl.BlockSpec((None, N_KV_HEADS, LANES, nc), lambda bi: (bi, 0, 0, 0)),
        ],
        out_shape=[
            jax.ShapeDtypeStruct((b, N_KV_HEADS, nc, LANES), BF16),
            jax.ShapeDtypeStruct((b, N_KV_HEADS, LANES, nc), BF16),
        ],
        compiler_params=_cparams("parallel"),
        name="ctx_kv",
    )(ctx, mod, ng, wkv, bd, kg)


def _attn_kernel(q_ref, kc_ref, vc_ref, k_ref, v_ref, o_ref, qs_ref, acc_ref, *m_scratch,
                 tq, tk, nk, online):
    n_h = q_ref.shape[0]
    for h in range(n_h):
        qs_ref[:, h * tq:(h + 1) * tq] = q_ref[h]

    if online:
        m_ref, = m_scratch
        m_ref[...] = jnp.full(m_ref.shape, -jnp.inf, F32)
        acc_ref[...] = jnp.zeros(acc_ref.shape, F32)

    def step(kx, vt, first):
        s = jnp.dot(kx, qs_ref[...], preferred_element_type=F32)
        if online:
            m_prev = m_ref[...]
            m_new = jnp.maximum(m_prev, jnp.max(s, axis=0, keepdims=True))
            p = jnp.exp2(s - m_new[0:1, :]).astype(BF16)
            acc_ref[...] = (jnp.exp2(m_prev - m_new)[0:1, :] * acc_ref[...]
                            + jnp.dot(vt, p, preferred_element_type=F32))
            m_ref[...] = m_new
        else:
            pv = jnp.dot(vt, jnp.exp2(s).astype(BF16), preferred_element_type=F32)
            if first:
                acc_ref[...] = pv
            else:
                acc_ref[...] += pv

    step(kc_ref[...], vc_ref[...], True)

    def body(c, carry):
        off = pl.multiple_of(c * tk, tk)
        step(k_ref[pl.ds(off, tk), :], v_ref[:, pl.ds(off, tk)], False)
        return carry

    lax.fori_loop(0, nk, body, 0)

    acc = acc_ref[...]
    r = acc[0:HEAD_DIM, :] / acc[HEAD_DIM:HEAD_DIM + 1, :]
    for p in range(n_h // 2):
        pair = jnp.concatenate([r[:, (2 * p) * tq:(2 * p + 1) * tq],
                                r[:, (2 * p + 1) * tq:(2 * p + 2) * tq]], axis=0)
        o_ref[:, p * LANES:(p + 1) * LANES] = pair.T.astype(BF16)


def _attention(q, kext_c, vext_c, kext, vext, tq, tk, online):
    b, _, _, s = q.shape
    nc = kext_c.shape[-2]
    n_h = N_HEADS // N_KV_HEADS
    cols = n_h * tq
    scratch = [pltpu.VMEM((LANES, cols), BF16), pltpu.VMEM((LANES, cols), F32)]
    if online:
        scratch.append(pltpu.VMEM((SUBLANES, cols), F32))
    return pl.pallas_call(
        functools.partial(_attn_kernel, tq=tq, tk=tk, nk=s // tk, online=online),
        grid=(b, N_KV_HEADS, s // tq),
        in_specs=[
            pl.BlockSpec((None, n_h, LANES, tq), lambda bi, g, i: (bi, g, 0, i)),
            pl.BlockSpec((None, None, nc, LANES), lambda bi, g, i: (bi, g, 0, 0)),
            pl.BlockSpec((None, None, LANES, nc), lambda bi, g, i: (bi, g, 0, 0)),
            pl.BlockSpec((None, None, s, LANES), lambda bi, g, i: (bi, g, 0, 0)),
            pl.BlockSpec((None, None, LANES, s), lambda bi, g, i: (bi, g, 0, 0)),
        ],
        out_specs=pl.BlockSpec((None, tq, n_h * HEAD_DIM), lambda bi, g, i: (bi, i, g)),
        out_shape=jax.ShapeDtypeStruct((b, s, ATTN_W), BF16),
        scratch_shapes=scratch,
        compiler_params=_cparams("parallel", "parallel", "parallel"),
        name="attention_online" if online else "attention",
    )(q, kext_c, vext_c, kext, vext)


def _l0_out_ffn_kernel(x_ref, attn_ref, conv_ref, mod_ref, ng_ref, wa_ref, wc_ref, wg_ref, wu_ref, wd_ref,
                       o_ref, *, n_chunks):
    y = (jnp.dot(attn_ref[...], wa_ref[...], preferred_element_type=F32)
         + jnp.dot(conv_ref[...], wc_ref[...], preferred_element_type=F32))
    x1 = x_ref[...] + mod_ref[2:3, :] * y
    f = _rms_mod(x1, ng_ref[...], mod_ref[3:4, :], mod_ref[4:5, :]).astype(BF16)
    acc = None
    for c in range(n_chunks):
        hg = jnp.dot(f, wg_ref[c], preferred_element_type=F32)
        hu = jnp.dot(f, wu_ref[c], preferred_element_type=F32)
        h = (hg * (1.0 / (1.0 + jnp.exp(-hg))) * hu).astype(BF16)
        part = jnp.dot(h, wd_ref[c], preferred_element_type=F32)
        acc = part if acc is None else acc + part
    o_ref[...] = x1 + mod_ref[5:6, :] * acc


def _l0_out_ffn(x, attn, conv, mod, ng, wa, wc, wg, wu, wd, tm):
    b, s, d = x.shape
    tile = lambda w: pl.BlockSpec((None, tm, w), lambda bi, i: (bi, i, 0))
    return pl.pallas_call(
        functools.partial(_l0_out_ffn_kernel, n_chunks=wg.shape[0]),
        grid=(b, s // tm),
        in_specs=[tile(d), tile(ATTN_W), tile(CONV_W),
                  pl.BlockSpec((None, SUBLANES, d), lambda bi, i: (bi, 0, 0)),
                  _const_spec((1, d)), _const_spec(wa.shape), _const_spec(wc.shape),
                  _const_spec(wg.shape), _const_spec(wu.shape), _const_spec(wd.shape)],
        out_specs=tile(d),
        out_shape=jax.ShapeDtypeStruct((b, s, d), F32),
        compiler_params=_cparams("parallel", "parallel"),
        name="l0_out_ffn",
    )(x, attn, conv, mod, ng, wa, wc, wg, wu, wd)


def _pool_kernel(x_ref, xp_ref, xn_ref, mod_ref, ng1_ref, ng2_ref, pw_ref, ps_ref, rwh_ref, rwl_ref, rb_ref,
                 x3_ref, f_ref, route_ref, cnt_ref, *, tm, seq):
    i = pl.program_id(1)
    ni = pl.num_programs(1)
    g = ng1_ref[...]
    shift, scale = mod_ref[0:1, :], mod_ref[1:2, :]
    x = x_ref[...]
    a = _rms_mod(x, g, shift, scale)
    a_before = _rms_mod(xp_ref[...], g, shift, scale) * (i > 0).astype(F32)
    a_after = _rms_mod(xn_ref[...], g, shift, scale) * (i < ni - 1).astype(F32)
    ext = jnp.concatenate([a_before, a, a_after], axis=0)
    n_ext = tm + 2 * POOL_HALO

    t = i * tm + lax.broadcasted_iota(jnp.int32, (tm, 1), 0)
    ys = []
    for gi, w in enumerate(POOL_WINDOWS):
        sl = slice(gi * POOL_GROUP_W, (gi + 1) * POOL_GROUP_W)
        e = ext[:, sl]
        acc = e + pltpu.roll(e, 1, 0)
        half = 1
        while 2 * half < w:
            acc = pltpu.roll(acc, n_ext - half, 0) + pltpu.roll(acc, half, 0)
            half *= 2
        win = acc[POOL_HALO:POOL_HALO + tm, :]
        cnt = jnp.minimum(t + w // 2, seq) - jnp.maximum(t - w // 2, 0)
        p = (win / cnt.astype(F32) - a[:, sl]).astype(BF16)
        ys.append(jnp.dot(p, pw_ref[gi], preferred_element_type=F32))
    y = jnp.concatenate(ys, axis=1) * ps_ref[...]

    x3 = x + mod_ref[2:3, :] * y
    x3_ref[...] = x3
    f = _rms_mod(x3, ng2_ref[...], mod_ref[3:4, :], mod_ref[4:5, :])
    f_hi = f.astype(BF16)
    f_ref[...] = f_hi
    f_lo = (f - f_hi.astype(F32)).astype(BF16)
    logits = (jnp.dot(f_hi, rwh_ref[...], preferred_element_type=F32)
              + jnp.dot(f_lo, rwh_ref[...], preferred_element_type=F32)
              + jnp.dot(f_hi, rwl_ref[...], preferred_element_type=F32)) + rb_ref[...]
    lane = lax.broadcasted_iota(jnp.int32, logits.shape, 1)
    neg = jnp.float32(-jnp.inf)
    logits = jnp.where(lane < N_EXPERTS, logits, neg)
    v1 = jnp.max(logits, axis=-1, keepdims=True)
    i1 = jnp.min(jnp.where(logits == v1, lane, LANES), axis=-1, keepdims=True)
    rest = jnp.where(lane == i1, neg, logits)
    v2 = jnp.max(rest, axis=-1, keepdims=True)
    i2 = jnp.min(jnp.where(rest == v2, lane, LANES), axis=-1, keepdims=True)
    e2 = jnp.exp(v2 - v1)
    p1 = 1.0 / (1.0 + e2)
    p2 = e2 / (1.0 + e2)
    i2 = jnp.where(p2 != 0.0, i2, -1)
    route_ref[...] = jnp.where(lane == 0, i1.astype(F32), jnp.where(
        lane == 1, i2.astype(F32), jnp.where(lane == 2, p1, jnp.where(lane == 3, p2, 0.0))))
    routed = jnp.where((lane == i1) | (lane == i2), 1.0, 0.0)
    for h in range(tm // DISPATCH_T):
        cnt = jnp.sum(routed[h * DISPATCH_T:(h + 1) * DISPATCH_T, :], axis=0, keepdims=True)
        cnt_ref[h:h + 1, :] = cnt.astype(jnp.int32)


def _pool_router(x2, mod, ng1, ng2, pw, ps, rw, rb, tm):
    b, s, d = x2.shape
    nt = s // tm
    hb = tm // POOL_HALO
    rwh = rw.astype(BF16)
    rwl = (rw - rwh.astype(F32)).astype(BF16)
    tile = lambda w: pl.BlockSpec((None, tm, w), lambda bi, i: (bi, i, 0))
    return pl.pallas_call(
        functools.partial(_pool_kernel, tm=tm, seq=s),
        grid=(b, nt),
        in_specs=[
            tile(d),
            pl.BlockSpec((None, POOL_HALO, d), lambda bi, i: (bi, jnp.maximum(i * hb - 1, 0), 0)),
            pl.BlockSpec((None, POOL_HALO, d),
                         lambda bi, i: (bi, jnp.minimum((i + 1) * hb, nt * hb - 1), 0)),
            pl.BlockSpec((None, SUBLANES, d), lambda bi, i: (bi, 0, 0)),
            _const_spec((1, d)), _const_spec((1, d)), _const_spec(pw.shape), _const_spec((1, d)),
            _const_spec(rw.shape), _const_spec(rw.shape), _const_spec(rb.shape),
        ],
        out_specs=[tile(d), tile(d), tile(LANES),
                   pl.BlockSpec((None, None, tm // DISPATCH_T, LANES), lambda bi, i: (bi, i, 0, 0))],
        out_shape=[jax.ShapeDtypeStruct((b, s, d), F32), jax.ShapeDtypeStruct((b, s, d), BF16),
                   jax.ShapeDtypeStruct((b, s, LANES), F32),
                   jax.ShapeDtypeStruct((b, nt, tm // DISPATCH_T, LANES), jnp.int32)],
        compiler_params=_cparams("parallel", "parallel"),
        name="pool_router",
    )(x2, x2, x2, mod, ng1, ng2, pw, ps, rwh, rwl, rb)


def _seg_copy(vmem_buf, slot, hbm, loc, dst, j, sem, to_hbm):
    v = vmem_buf.at[slot, pl.ds(pl.multiple_of(loc + j * SEG_ALIGN, SEG_ALIGN), SEG_ALIGN)]
    h = hbm.at[pl.ds(pl.multiple_of(dst + j * SEG_ALIGN, SEG_ALIGN), SEG_ALIGN)]
    return pltpu.make_async_copy(v, h, sem) if to_hbm else pltpu.make_async_copy(h, v, sem)


def _for_each_chunk(loc_s, gdst_s, nch_s, tile, fn):
    for e in range(N_EXPERTS):
        k = tile * N_EXPERTS + e
        loc, dst = loc_s[k], gdst_s[k]

        def body(j, carry, loc=loc, dst=dst):
            fn(loc, dst, j)
            return carry

        lax.fori_loop(0, nch_s[k], body, 0)


def _dispatch_kernel(loc_s, gdst_s, nch_s, tail_s, f_ref, route_ref, locrow_ref, tril_ref,
                     xs_hbm, d_ref, sbuf, zbuf, sem, zsem, *, g):
    i = pl.program_id(0)
    n_steps = pl.num_programs(0)
    t = DISPATCH_T
    base, other = (i % 2) * g, (1 - i % 2) * g
    row = lax.broadcasted_iota(jnp.int32, (DISPATCH_ROWS, t), 0).astype(F32)
    for sub in range(g):
        rows = slice(sub * t, (sub + 1) * t)
        route = route_ref[rows, :]
        lane = lax.broadcasted_iota(jnp.int32, route.shape, 1).astype(F32)
        first, second = lane == route[:, 0:1], lane == route[:, 1:2]
        onehot = jnp.where(first | second, 1.0, 0.0).astype(BF16)
        earlier = jnp.dot(tril_ref[...], onehot, preferred_element_type=F32)
        row_of = earlier + locrow_ref[sub]
        d_first = jnp.sum(jnp.where(first, row_of, 0.0), axis=1, keepdims=True)
        d_second = jnp.sum(jnp.where(second, row_of, 0.0), axis=1, keepdims=True)
        d_second = jnp.where(route[:, 1:2] < 0.0, -1.0, d_second)
        d = jnp.where(lane == 0.0, d_first, jnp.where(lane == 1.0, d_second, -1.0))
        d_ref[rows, :] = d
        dt = d.T
        perm = jnp.where((row == dt[0:1, :]) | (row == dt[1:2, :]), 1.0, 0.0)
        sbuf[base + sub] = jnp.dot(perm.astype(BF16), f_ref[rows, :], preferred_element_type=F32).astype(BF16)

    def copies(tile, slot, issue):
        def run(loc, dst, j):
            c = _seg_copy(sbuf, slot, xs_hbm, loc, dst, j, sem.at[slot], True)
            c.start() if issue else c.wait()
        _for_each_chunk(loc_s, gdst_s, nch_s, tile, run)

    for sub in range(g):
        copies(i * g + sub, base + sub, True)

    @pl.when(i > 0)
    def _():
        for sub in range(g):
            copies((i - 1) * g + sub, other + sub, False)

    @pl.when(i == n_steps - 1)
    def _():
        zbuf[...] = jnp.zeros(zbuf.shape, BF16)
        n_blk = xs_hbm.shape[0] // EXPERT_RB

        def gap_copy(e, j):
            dst = pl.multiple_of(tail_s[e] + j * SEG_ALIGN, SEG_ALIGN)
            return pltpu.make_async_copy(zbuf.at[pl.ds(0, SEG_ALIGN)], xs_hbm.at[pl.ds(dst, SEG_ALIGN)], zsem)

        def blk_copy(blk):
            dst = pl.multiple_of(blk * EXPERT_RB, EXPERT_RB)
            return pltpu.make_async_copy(zbuf, xs_hbm.at[pl.ds(dst, EXPERT_RB)], zsem)

        def fill(issue):
            def run(copy):
                copy.start() if issue else copy.wait()

            for e in range(N_EXPERTS):
                def gap_body(j, carry, e=e):
                    run(gap_copy(e, j))
                    return carry

                lax.fori_loop(0, tail_s[N_EXPERTS + e], gap_body, 0)

            def blk_body(blk, carry):
                run(blk_copy(blk))
                return carry

            lax.fori_loop(tail_s[2 * N_EXPERTS], n_blk, blk_body, 0)

        fill(True)
        for sub in range(g):
            copies(i * g + sub, base + sub, False)
        fill(False)


def _dispatch(f, route, loc, gdst, nch, tail, n_blk, g):
    n, d = f.shape
    t = DISPATCH_T
    locrow = jnp.pad(loc.astype(F32), ((0, 0), (0, LANES - N_EXPERTS)))[:, None, :]
    tril = (jnp.arange(t)[:, None] > jnp.arange(t)[None, :]).astype(BF16)
    grid_spec = pltpu.PrefetchScalarGridSpec(
        num_scalar_prefetch=4,
        grid=(n // (g * t),),
        in_specs=[pl.BlockSpec((g * t, d), lambda i, *_: (i, 0)),
                  pl.BlockSpec((g * t, LANES), lambda i, *_: (i, 0)),
                  pl.BlockSpec((g, 1, LANES), lambda i, *_: (i, 0, 0)),
                  pl.BlockSpec((t, t), lambda i, *_: (0, 0))],
        out_specs=[pl.BlockSpec(memory_space=pl.ANY),
                   pl.BlockSpec((g * t, LANES), lambda i, *_: (i, 0))],
        scratch_shapes=[pltpu.VMEM((2 * g, DISPATCH_ROWS, d), BF16),
                        pltpu.VMEM((EXPERT_RB, d), BF16),
                        pltpu.SemaphoreType.DMA((2 * g,)),
                        pltpu.SemaphoreType.DMA(())],
    )
    return pl.pallas_call(
        functools.partial(_dispatch_kernel, g=g),
        grid_spec=grid_spec,
        out_shape=[jax.ShapeDtypeStruct((n_blk * EXPERT_RB, d), BF16),
                   jax.ShapeDtypeStruct((n, LANES), F32)],
        compiler_params=_cparams("arbitrary"),
        name="dispatch",
    )(loc.reshape(-1), gdst.reshape(-1), nch.reshape(-1), tail, f, route, locrow, tril)


def _expert_kernel(exp_s, nvalid_s, xs_ref, wgu_ref, wd_ref, ys_ref):
    used = pl.program_id(0) < nvalid_s[0]

    @pl.when(used)
    def _():
        dff = wd_ref.shape[0]
        z = jnp.dot(xs_ref[...], wgu_ref[...], preferred_element_type=F32)
        hg, hu = z[:, 0:dff], z[:, dff:2 * dff]
        h = (hg * (1.0 / (1.0 + jnp.exp(-hg))) * hu).astype(BF16)
        ys_ref[...] = jnp.dot(h, wd_ref[...], preferred_element_type=F32).astype(BF16)

    @pl.when(jnp.logical_not(used))
    def _():
        ys_ref[...] = jnp.zeros(ys_ref.shape, BF16)


def _experts(xs, blk_exp, n_valid, wgu, wd):
    rows, d = xs.shape
    dff = wd.shape[1]
    grid_spec = pltpu.PrefetchScalarGridSpec(
        num_scalar_prefetch=2,
        grid=(rows // EXPERT_RB,),
        in_specs=[pl.BlockSpec((EXPERT_RB, d), lambda b, exp_s, nv: (b, 0)),
                  pl.BlockSpec((None, d, 2 * dff), lambda b, exp_s, nv: (exp_s[b], 0, 0)),
                  pl.BlockSpec((None, dff, d), lambda b, exp_s, nv: (exp_s[b], 0, 0))],
        out_specs=pl.BlockSpec((EXPERT_RB, d), lambda b, exp_s, nv: (b, 0)),
    )
    return pl.pallas_call(
        _expert_kernel,
        grid_spec=grid_spec,
        out_shape=jax.ShapeDtypeStruct((rows, d), BF16),
        compiler_params=_cparams("arbitrary"),
        name="experts",
    )(blk_exp, n_valid, xs, wgu, wd)


def _combine_kernel(loc_s, gdst_s, nch_s, x_ref, route_ref, d_ref, mod_ref, fg_ref, ys_hbm,
                    o_ref, ybuf, sem, *, g):
    i = pl.program_id(0)
    n_steps = pl.num_programs(0)
    t = DISPATCH_T
    base, other = (i % 2) * g, (1 - i % 2) * g

    def copies(tile, slot, issue):
        def run(loc, dst, j):
            c = _seg_copy(ybuf, slot, ys_hbm, loc, dst, j, sem.at[slot], False)
            c.start() if issue else c.wait()
        _for_each_chunk(loc_s, gdst_s, nch_s, tile, run)

    @pl.when(i == 0)
    def _():
        ybuf[...] = jnp.zeros(ybuf.shape, BF16)
        for sub in range(g):
            copies(sub, base + sub, True)

    @pl.when(i + 1 < n_steps)
    def _():
        for sub in range(g):
            copies((i + 1) * g + sub, other + sub, True)

    for sub in range(g):
        copies(i * g + sub, base + sub, False)
    lane = lax.broadcasted_iota(jnp.int32, (t, DISPATCH_ROWS), 1).astype(F32)
    for sub in range(g):
        rows = slice(sub * t, (sub + 1) * t)
        route, d = route_ref[rows, :], d_ref[rows, :]
        gmat = jnp.where(lane == d[:, 0:1], route[:, 2:3], jnp.where(lane == d[:, 1:2], route[:, 3:4], 0.0))
        moe = jnp.dot(gmat.astype(BF16), ybuf[base + sub], preferred_element_type=F32)
        x4 = x_ref[rows, :] + mod_ref[5:6, :] * moe
        ms = jnp.mean(x4 * x4, axis=-1, keepdims=True)
        o_ref[rows, :] = x4 * lax.rsqrt(ms + EPS) * fg_ref[...]


def _combine(x3, route, dmap, ys, mod, fg, loc, gdst, nch, steps_per_batch, g):
    n, d = x3.shape
    t = DISPATCH_T
    grid_spec = pltpu.PrefetchScalarGridSpec(
        num_scalar_prefetch=3,
        grid=(n // (g * t),),
        in_specs=[pl.BlockSpec((g * t, d), lambda i, *_: (i, 0)),
                  pl.BlockSpec((g * t, LANES), lambda i, *_: (i, 0)),
                  pl.BlockSpec((g * t, LANES), lambda i, *_: (i, 0)),
                  pl.BlockSpec((None, SUBLANES, d), lambda i, *_: (i // steps_per_batch, 0, 0)),
                  pl.BlockSpec((1, d), lambda i, *_: (0, 0)),
                  pl.BlockSpec(memory_space=pl.ANY)],
        out_specs=pl.BlockSpec((g * t, d), lambda i, *_: (i, 0)),
        scratch_shapes=[pltpu.VMEM((2 * g, DISPATCH_ROWS, d), BF16),
                        pltpu.SemaphoreType.DMA((2 * g,))],
    )
    return pl.pallas_call(
        functools.partial(_combine_kernel, g=g),
        grid_spec=grid_spec,
        out_shape=jax.ShapeDtypeStruct((n, d), F32),
        compiler_params=_cparams("arbitrary"),
        name="combine",
    )(loc.reshape(-1), gdst.reshape(-1), nch.reshape(-1), x3, route, dmap, mod, fg, ys)


def _routing_tables(cnt, n_tokens):
    n_tiles = cnt.shape[0]
    i32 = lambda a: a.astype(jnp.int32)
    npad = (cnt + SEG_ALIGN - 1) // SEG_ALIGN * SEG_ALIGN
    loc = jnp.cumsum(npad, axis=1) - npad
    tot = jnp.sum(npad, axis=0)
    nblk = (tot + EXPERT_RB - 1) // EXPERT_RB
    ends = jnp.cumsum(nblk)
    base = (ends - nblk) * EXPERT_RB
    gdst = base[None, :] + jnp.cumsum(npad, axis=0) - npad
    nch = npad // SEG_ALIGN
    n_blk = (2 * n_tokens + (SEG_ALIGN - 1) * N_EXPERTS * n_tiles) // EXPERT_RB + N_EXPERTS
    blk_exp = jnp.minimum(jnp.searchsorted(ends, jnp.arange(n_blk), side="right"), N_EXPERTS - 1)
    tail = jnp.concatenate([base + tot, (nblk * EXPERT_RB - tot) // SEG_ALIGN, ends[-1:]])
    return n_blk, i32(loc), i32(gdst), i32(nch), i32(tail), i32(blk_exp), i32(ends[-1:])


def _rope_tables(n):
    t = jnp.arange(n)
    pos = jnp.stack([t // GRID_W, t % GRID_W], axis=1).astype(F32)
    freqs = ROPE_BASE ** (-jnp.arange(0, ROPE_AXIS_DIM, 2, dtype=F32) / ROPE_AXIS_DIM)
    ang = pos[:, :, None] * freqs
    ang = jnp.broadcast_to(ang[:, :, None, :], (n, 2, 2, ROPE_AXIS_DIM // 2))
    sign = jnp.array([-1.0, 1.0], F32)[None, None, :, None]
    cos = jnp.cos(ang).reshape(n, HEAD_DIM)
    sin = (jnp.sin(ang) * sign).reshape(n, HEAD_DIM)
    return jnp.tile(cos, (1, LANES // HEAD_DIM)), jnp.tile(sin, (1, LANES // HEAD_DIM))


def _tile_sizes(s):
    return min(512, s), min(512, s), min(2048, s)


def _pad_rows(m, rows):
    return jnp.pad(m, ((0, 0),) * (m.ndim - 2) + ((0, rows - m.shape[-2]), (0, 0)))


def kernel(x, c, ctx, c_ctx, w_mod, b_mod, norm_g, final_norm_g, w_mix_in, q_norm_g, k_norm_g, conv_w, w_mix_out, ffn_w_gate, ffn_w_up, ffn_w_down, pool_w, pool_scale, router_w, router_b, exp_w_gate, exp_w_up, exp_w_down):
    b, s, d = x.shape
    tm, tq, tk = _tile_sizes(s)

    rows = -(-(b + 1) // SUBLANES) * SUBLANES
    cvec = _pad_rows(jnp.concatenate([c, c_ctx[None, :]], axis=0), rows)
    mods = _adaln(cvec, w_mod, b_mod).reshape(w_mod.shape[0], rows, N_MOD, d)
    mod_lat = [_pad_rows(mods[l, :b], SUBLANES) for l in range(2)]
    mod_ctx0 = _pad_rows(mods[0, b], SUBLANES)

    w_in = w_mix_in[0].astype(BF16)
    cuts = (ATTN_W, ATTN_W + KV_W, ATTN_W + 2 * KV_W)
    wq, wkv, wc = w_in[:, :cuts[0]], w_in[:, cuts[0]:cuts[2]], w_in[:, cuts[2]:]
    head = jnp.arange(ATTN_W) // HEAD_DIM
    bd = ((head[:, None] == head[None, :]).astype(F32) / HEAD_DIM).astype(BF16)
    qg = jnp.tile(q_norm_g[0], N_HEADS)[None, :]
    kg = jnp.tile(k_norm_g[0], N_KV_HEADS)[None, :]
    cw = _pad_rows(conv_w[0], SUBLANES)
    cos, sin = _rope_tables(s)
    ng0a, ng0b = norm_g[0, 0][None, :], norm_g[0, 1][None, :]

    shift = (Q_SCALE * HEAD_DIM * SHIFT_MARGIN) * jnp.max(jnp.abs(q_norm_g[0])) * jnp.max(jnp.abs(k_norm_g[0]))
    qbias = jnp.where(jnp.arange(LANES - HEAD_DIM)[:, None] == 0, -shift, 0.0).astype(F32) * jnp.ones((1, tm), F32)
    q, kext, vext, conv = _l0_in(x, mod_lat[0], ng0a, wq, wkv, wc, bd, qg, kg, cw, cos, sin, qbias, tm)
    kext_c, vext_c = _ctx_kv(ctx, mod_ctx0, ng0a, wkv, bd, kg)
    attn = lax.cond(
        shift <= MAX_FIXED_SHIFT,
        lambda *a: _attention(*a, tq, tk, False),
        lambda *a: _attention(*a, tq, tk, True),
        q, kext_c, vext_c, kext, vext)

    w_out = w_mix_out[0].astype(BF16)
    n_chunks = 1
    dff = ffn_w_gate.shape[-1]
    split_cols = lambda w: w.astype(BF16).reshape(d, n_chunks, dff // n_chunks).transpose(1, 0, 2)
    wd0 = ffn_w_down[0].astype(BF16).reshape(n_chunks, dff // n_chunks, d)
    x2 = _l0_out_ffn(x, attn, conv, mod_lat[0], ng0b, w_out[:ATTN_W], w_out[ATTN_W:],
                     split_cols(ffn_w_gate[0]), split_cols(ffn_w_up[0]), wd0, tm)

    rw = jnp.pad(router_w[0], ((0, 0), (0, LANES - N_EXPERTS)))
    rb = jnp.pad(router_b[0], (0, LANES - N_EXPERTS))[None, :]
    x3, f2, route, cnt = _pool_router(x2, mod_lat[1], norm_g[1, 0][None, :], norm_g[1, 1][None, :],
                                      pool_w[0].astype(BF16), pool_scale[0][None, :], rw, rb, tm)
    n = b * s
    n_blk, loc, gdst, nch, tail, blk_exp, n_valid = _routing_tables(
        cnt.reshape(n // DISPATCH_T, LANES)[:, :N_EXPERTS], n)
    route = route.reshape(n, LANES)
    tiles_per_batch = s // DISPATCH_T
    g = max(k for k in (4, 2, 1) if tiles_per_batch % k == 0)
    xs, dmap = _dispatch(f2.reshape(n, d), route, loc, gdst, nch, tail, n_blk, g)
    wgu = jnp.concatenate([exp_w_gate[0].astype(BF16), exp_w_up[0].astype(BF16)], axis=-1)
    ys = _experts(xs, blk_exp, n_valid, wgu, exp_w_down[0].astype(BF16))
    out = _combine(x3.reshape(n, d), route, dmap, ys, mod_lat[1], final_norm_g[None, :],
                   loc, gdst, nch, tiles_per_batch // g, g)
    return out.reshape(b, s, d)
```

```python
import functools
import math

import jax
import jax.numpy as jnp
from jax import lax
from jax.experimental import pallas as pl
from jax.experimental.pallas import tpu as pltpu

F32 = jnp.float32
BF16 = jnp.bfloat16

D_MODEL = 1024
GRID_W = 64
N_HEADS = 8
N_KV_HEADS = 2
HEAD_DIM = 64
ATTN_W = N_HEADS * HEAD_DIM
KV_W = N_KV_HEADS * HEAD_DIM
CONV_W = D_MODEL - ATTN_W
ROPE_AXIS_DIM = HEAD_DIM // 2
ROPE_BASE = 10000.0
POOL_WINDOWS = (2, 4, 8, 16)
POOL_GROUP_W = D_MODEL // len(POOL_WINDOWS)
POOL_HALO = 8
N_EXPERTS = 8
DISPATCH_T = 256
SEG_ALIGN = 16
DISPATCH_ROWS = 2 * DISPATCH_T + N_EXPERTS * SEG_ALIGN
EXPERT_RB = 512
N_MOD = 6
EPS = 1e-6

LANES = 128
SUBLANES = 8
VMEM_LIMIT = 56 * 1024 * 1024

Q_SCALE = (HEAD_DIM ** -0.5) * math.log2(math.e)
SHIFT_MARGIN = 1.01
MAX_FIXED_SHIFT = 48.0


def _cparams(*sem):
    return pltpu.CompilerParams(dimension_semantics=sem, vmem_limit_bytes=VMEM_LIMIT)


def _const_spec(shape):
    nd = len(shape)
    return pl.BlockSpec(shape, lambda *_: (0,) * nd, pipeline_mode=pl.Buffered(1))


def _rms_mod(x, g, shift, scale):
    ms = jnp.mean(x * x, axis=-1, keepdims=True)
    return x * lax.rsqrt(ms + EPS) * (g * (1.0 + scale)) + shift


def _head_rms(z, bd, g):
    ms = jnp.dot((z * z).astype(BF16), bd, preferred_element_type=F32)
    return z * lax.rsqrt(ms + EPS) * g


def _rope128(x, cos, sin_signed):
    lane = lax.broadcasted_iota(jnp.int32, x.shape, 1)
    partner = jnp.where((lane & 16) == 0, pltpu.roll(x, LANES - 16, 1), pltpu.roll(x, 16, 1))
    return x * cos + partner * sin_signed


def _adaln_kernel(c_ref, w_ref, b_ref, o_ref):
    c = c_ref[...]
    s = c * (1.0 / (1.0 + jnp.exp(-c)))
    o_ref[...] = jnp.dot(s, w_ref[...], preferred_element_type=F32,
                         precision=lax.Precision.HIGHEST) + b_ref[...]


def _adaln(cvec, w_mod, b_mod):
    depth, d, n = w_mod.shape
    rows = cvec.shape[0]
    tn = 1536
    return pl.pallas_call(
        _adaln_kernel,
        grid=(depth, n // tn),
        in_specs=[pl.BlockSpec((rows, d), lambda l, j: (0, 0)),
                  pl.BlockSpec((None, d, tn), lambda l, j: (l, 0, j)),
                  pl.BlockSpec((None, 1, tn), lambda l, j: (l, 0, j))],
        out_specs=pl.BlockSpec((None, rows, tn), lambda l, j: (l, 0, j)),
        out_shape=jax.ShapeDtypeStruct((depth, rows, n), F32),
        compiler_params=_cparams("parallel", "parallel"),
        name="adaln",
    )(cvec, w_mod, b_mod.reshape(depth, 1, n))


def _kv_outputs(zk, zv, kext_ref, vext_ref):
    rows = zk.shape[0]
    lane = lax.broadcasted_iota(jnp.int32, zk.shape, 1)
    pad = jnp.where(lane == HEAD_DIM, 1.0, 0.0)
    zk_swapped = pltpu.roll(zk, HEAD_DIM, 1)
    vt = zv.T
    sub = lax.broadcasted_iota(jnp.int32, (LANES - HEAD_DIM, rows), 0)
    ones_row = jnp.where(sub == 0, 1.0, 0.0)
    for g in range(N_KV_HEADS):
        kg = zk if g == 0 else zk_swapped
        kext_ref[g] = jnp.where(lane < HEAD_DIM, kg, pad).astype(BF16)
        vext_ref[g] = jnp.concatenate([vt[g * HEAD_DIM:(g + 1) * HEAD_DIM, :], ones_row], axis=0).astype(BF16)


def _l0_in_kernel(x_ref, xp_ref, xn_ref, mod_ref, ng_ref, wq_ref, wkv_ref, wc_ref,
                  bd_ref, qg_ref, kg_ref, cw_ref, cos_ref, sin_ref, qb_ref,
                  q_ref, kext_ref, vext_ref, conv_ref, *, tm):
    i = pl.program_id(1)
    ni = pl.num_programs(1)
    g = ng_ref[...]
    shift, scale = mod_ref[0:1, :], mod_ref[1:2, :]
    a = _rms_mod(x_ref[...], g, shift, scale).astype(BF16)
    cos, sin = cos_ref[...], sin_ref[...]

    zq = jnp.dot(a, wq_ref[...], preferred_element_type=F32)
    qn = _head_rms(zq, bd_ref[...], qg_ref[...])
    qbias = qb_ref[...]
    for c in range(ATTN_W // LANES):
        rt = (_rope128(qn[:, c * LANES:(c + 1) * LANES], cos, sin) * Q_SCALE).T
        for hh in range(2):
            q_ref[2 * c + hh] = jnp.concatenate(
                [rt[hh * HEAD_DIM:(hh + 1) * HEAD_DIM, :], qbias], axis=0).astype(BF16)

    zkv = jnp.dot(a, wkv_ref[...], preferred_element_type=F32)
    kn = _rope128(_head_rms(zkv[:, 0:KV_W], bd_ref[0:KV_W, 0:KV_W], kg_ref[...]), cos, sin)
    _kv_outputs(kn, zkv[:, KV_W:2 * KV_W], kext_ref, vext_ref)

    zc = jnp.dot(a, wc_ref[...], preferred_element_type=F32)
    gb, w = zc[:, 0:CONV_W], zc[:, CONV_W:2 * CONV_W] * zc[:, 2 * CONV_W:3 * CONV_W]
    halo = jnp.concatenate([xp_ref[...], xn_ref[...]], axis=0)
    ah = _rms_mod(halo, g, shift, scale).astype(BF16)
    zh = jnp.dot(ah, wc_ref[:, CONV_W:3 * CONV_W], preferred_element_type=F32)
    wh = zh[:, 0:CONV_W] * zh[:, CONV_W:2 * CONV_W]
    w_before = wh[SUBLANES - 1:SUBLANES, :] * (i > 0).astype(F32)
    w_after = wh[SUBLANES:SUBLANES + 1, :] * (i < ni - 1).astype(F32)
    row = lax.broadcasted_iota(jnp.int32, w.shape, 0)
    w_prev = jnp.where(row == 0, w_before, pltpu.roll(w, 1, 0))
    w_next = jnp.where(row == tm - 1, w_after, pltpu.roll(w, tm - 1, 0))
    y = gb * (w_prev * cw_ref[0:1, :] + w * cw_ref[1:2, :] + w_next * cw_ref[2:3, :])
    conv_ref[...] = y.astype(BF16)


def _l0_in(x, mod, ng, wq, wkv, wc, bd, qg, kg, cw, cos, sin, qbias, tm):
    b, s, d = x.shape
    nt = s // tm
    hb = tm // SUBLANES
    return pl.pallas_call(
        functools.partial(_l0_in_kernel, tm=tm),
        grid=(b, nt),
        in_specs=[
            pl.BlockSpec((None, tm, d), lambda bi, i: (bi, i, 0)),
            pl.BlockSpec((None, SUBLANES, d), lambda bi, i: (bi, jnp.maximum(i * hb - 1, 0), 0)),
            pl.BlockSpec((None, SUBLANES, d),
                         lambda bi, i: (bi, jnp.minimum((i + 1) * hb, nt * hb - 1), 0)),
            pl.BlockSpec((None, SUBLANES, d), lambda bi, i: (bi, 0, 0)),
            _const_spec((1, d)),
            _const_spec(wq.shape), _const_spec(wkv.shape),
            _const_spec(wc.shape), _const_spec(bd.shape),
            _const_spec(qg.shape), _const_spec(kg.shape), _const_spec(cw.shape),
            pl.BlockSpec((tm, LANES), lambda bi, i: (i, 0)),
            pl.BlockSpec((tm, LANES), lambda bi, i: (i, 0)),
            _const_spec(qbias.shape),
        ],
        out_specs=[
            pl.BlockSpec((None, N_HEADS, LANES, tm), lambda bi, i: (bi, 0, 0, i)),
            pl.BlockSpec((None, N_KV_HEADS, tm, LANES), lambda bi, i: (bi, 0, i, 0)),
            pl.BlockSpec((None, N_KV_HEADS, LANES, tm), lambda bi, i: (bi, 0, 0, i)),
            pl.BlockSpec((None, tm, CONV_W), lambda bi, i: (bi, i, 0)),
        ],
        out_shape=[
            jax.ShapeDtypeStruct((b, N_HEADS, LANES, s), BF16),
            jax.ShapeDtypeStruct((b, N_KV_HEADS, s, LANES), BF16),
            jax.ShapeDtypeStruct((b, N_KV_HEADS, LANES, s), BF16),
            jax.ShapeDtypeStruct((b, s, CONV_W), BF16),
        ],
        compiler_params=_cparams("parallel", "parallel"),
        name="l0_in",
    )(x, x, x, mod, ng, wq, wkv, wc, bd, qg, kg, cw, cos, sin, qbias)


def _ctx_kv_kernel(x_ref, mod_ref, ng_ref, wkv_ref, bd_ref, kg_ref, kext_ref, vext_ref):
    a = _rms_mod(x_ref[...], ng_ref[...], mod_ref[0:1, :], mod_ref[1:2, :]).astype(BF16)
    zkv = jnp.dot(a, wkv_ref[...], preferred_element_type=F32)
    kn = _head_rms(zkv[:, 0:KV_W], bd_ref[0:KV_W, 0:KV_W], kg_ref[...])
    _kv_outputs(kn, zkv[:, KV_W:2 * KV_W], kext_ref, vext_ref)


def _ctx_kv(ctx, mod, ng, wkv, bd, kg):
    b, nc, d = ctx.shape
    return pl.pallas_call(
        _ctx_kv_kernel,
        grid=(b,),
        in_specs=[
            pl.BlockSpec((None, nc, d), lambda bi: (bi, 0, 0)),
            _const_spec(mod.shape), _const_spec((1, d)),
            _const_spec(wkv.shape), _const_spec(bd.shape), _const_spec(kg.shape),
        ],
        out_specs=[
            pl.BlockSpec((None, N_KV_HEADS, nc, LANES), lambda bi: (bi, 0, 0, 0)),
            pl.BlockSpec((None, N_KV_HEADS, LANES, nc), lambda bi: (bi, 0, 0, 0)),
        ],
        out_shape=[
            jax.ShapeDtypeStruct((b, N_KV_HEADS, nc, LANES), BF16),
            jax.ShapeDtypeStruct((b, N_KV_HEADS, LANES, nc), BF16),
        ],
        compiler_params=_cparams("parallel"),
        name="ctx_kv",
    )(ctx, mod, ng, wkv, bd, kg)


def _attn_kernel(q_ref, kc_ref, vc_ref, k_ref, v_ref, o_ref, qs_ref, acc_ref, *m_scratch,
                 tq, tk, nk, online):
    n_h = q_ref.shape[0]
    for h in range(n_h):
        qs_ref[:, h * tq:(h + 1) * tq] = q_ref[h]

    if online:
        m_ref, = m_scratch
        m_ref[...] = jnp.full(m_ref.shape, -jnp.inf, F32)
        acc_ref[...] = jnp.zeros(acc_ref.shape, F32)

    def step(kx, vt, first):
        s = jnp.dot(kx, qs_ref[...], preferred_element_type=F32)
        if online:
            m_prev = m_ref[...]
            m_new = jnp.maximum(m_prev, jnp.max(s, axis=0, keepdims=True))
            p = jnp.exp2(s - m_new[0:1, :]).astype(BF16)
            acc_ref[...] = (jnp.exp2(m_prev - m_new)[0:1, :] * acc_ref[...]
                            + jnp.dot(vt, p, preferred_element_type=F32))
            m_ref[...] = m_new
        else:
            pv = jnp.dot(vt, jnp.exp2(s).astype(BF16), preferred_element_type=F32)
            if first:
                acc_ref[...] = pv
            else:
                acc_ref[...] += pv

    step(kc_ref[...], vc_ref[...], True)

    def body(c, carry):
        off = pl.multiple_of(c * tk, tk)
        step(k_ref[pl.ds(off, tk), :], v_ref[:, pl.ds(off, tk)], False)
        return carry

    lax.fori_loop(0, nk, body, 0)

    acc = acc_ref[...]
    r = acc[0:HEAD_DIM, :] / acc[HEAD_DIM:HEAD_DIM + 1, :]
    for p in range(n_h // 2):
        pair = jnp.concatenate([r[:, (2 * p) * tq:(2 * p + 1) * tq],
                                r[:, (2 * p + 1) * tq:(2 * p + 2) * tq]], axis=0)
        o_ref[:, p * LANES:(p + 1) * LANES] = pair.T.astype(BF16)


def _attention(q, kext_c, vext_c, kext, vext, tq, tk, online):
    b, _, _, s = q.shape
    nc = kext_c.shape[-2]
    n_h = N_HEADS // N_KV_HEADS
    cols = n_h * tq
    scratch = [pltpu.VMEM((LANES, cols), BF16), pltpu.VMEM((LANES, cols), F32)]
    if online:
        scratch.append(pltpu.VMEM((SUBLANES, cols), F32))
    return pl.pallas_call(
        functools.partial(_attn_kernel, tq=tq, tk=tk, nk=s // tk, online=online),
        grid=(b, N_KV_HEADS, s // tq),
        in_specs=[
            pl.BlockSpec((None, n_h, LANES, tq), lambda bi, g, i: (bi, g, 0, i)),
            pl.BlockSpec((None, None, nc, LANES), lambda bi, g, i: (bi, g, 0, 0)),
            pl.BlockSpec((None, None, LANES, nc), lambda bi, g, i: (bi, g, 0, 0)),
            pl.BlockSpec((None, None, s, LANES), lambda bi, g, i: (bi, g, 0, 0)),
            pl.BlockSpec((None, None, LANES, s), lambda bi, g, i: (bi, g, 0, 0)),
        ],
        out_specs=pl.BlockSpec((None, tq, n_h * HEAD_DIM), lambda bi, g, i: (bi, i, g)),
        out_shape=jax.ShapeDtypeStruct((b, s, ATTN_W), BF16),
        scratch_shapes=scratch,
        compiler_params=_cparams("parallel", "parallel", "parallel"),
        name="attention_online" if online else "attention",
    )(q, kext_c, vext_c, kext, vext)


def _l0_out_ffn_kernel(x_ref, attn_ref, conv_ref, mod_ref, ng_ref, wa_ref, wc_ref, wg_ref, wu_ref, wd_ref,
                       o_ref, *, n_chunks):
    y = (jnp.dot(attn_ref[...], wa_ref[...], preferred_element_type=F32)
         + jnp.dot(conv_ref[...], wc_ref[...], preferred_element_type=F32))
    x1 = x_ref[...] + mod_ref[2:3, :] * y
    f = _rms_mod(x1, ng_ref[...], mod_ref[3:4, :], mod_ref[4:5, :]).astype(BF16)
    acc = None
    for c in range(n_chunks):
        hg = jnp.dot(f, wg_ref[c], preferred_element_type=F32)
        hu = jnp.dot(f, wu_ref[c], preferred_element_type=F32)
        h = (hg * (1.0 / (1.0 + jnp.exp(-hg))) * hu).astype(BF16)
        part = jnp.dot(h, wd_ref[c], preferred_element_type=F32)
        acc = part if acc is None else acc + part
    o_ref[...] = x1 + mod_ref[5:6, :] * acc


def _l0_out_ffn(x, attn, conv, mod, ng, wa, wc, wg, wu, wd, tm):
    b, s, d = x.shape
    tile = lambda w: pl.BlockSpec((None, tm, w), lambda bi, i: (bi, i, 0))
    return pl.pallas_call(
        functools.partial(_l0_out_ffn_kernel, n_chunks=wg.shape[0]),
        grid=(b, s // tm),
        in_specs=[tile(d), tile(ATTN_W), tile(CONV_W),
                  pl.BlockSpec((None, SUBLANES, d), lambda bi, i: (bi, 0, 0)),
                  _const_spec((1, d)), _const_spec(wa.shape), _const_spec(wc.shape),
                  _const_spec(wg.shape), _const_spec(wu.shape), _const_spec(wd.shape)],
        out_specs=tile(d),
        out_shape=jax.ShapeDtypeStruct((b, s, d), F32),
        compiler_params=_cparams("parallel", "parallel"),
        name="l0_out_ffn",
    )(x, attn, conv, mod, ng, wa, wc, wg, wu, wd)


def _pool_kernel(x_ref, xp_ref, xn_ref, mod_ref, ng1_ref, ng2_ref, pw_ref, ps_ref, rw_ref, rb_ref,
                 x3_ref, f_ref, route_ref, cnt_ref, *, tm, seq):
    i = pl.program_id(1)
    ni = pl.num_programs(1)
    g = ng1_ref[...]
    shift, scale = mod_ref[0:1, :], mod_ref[1:2, :]
    x = x_ref[...]
    a = _rms_mod(x, g, shift, scale)
    a_before = _rms_mod(xp_ref[...], g, shift, scale) * (i > 0).astype(F32)
    a_after = _rms_mod(xn_ref[...], g, shift, scale) * (i < ni - 1).astype(F32)
    ext = jnp.concatenate([a_before, a, a_after], axis=0)

    t = i * tm + lax.broadcasted_iota(jnp.int32, (tm, 1), 0)
    ys = []
    for gi, w in enumerate(POOL_WINDOWS):
        sl = slice(gi * POOL_GROUP_W, (gi + 1) * POOL_GROUP_W)
        e = ext[:, sl]
        acc, span = e, 1
        while span < w:
            acc = acc + pltpu.roll(acc, span, 0)
            span *= 2
        win = acc[POOL_HALO + w // 2 - 1:POOL_HALO + w // 2 - 1 + tm, :]
        cnt = jnp.minimum(t + w // 2, seq) - jnp.maximum(t - w // 2, 0)
        p = (win / cnt.astype(F32) - a[:, sl]).astype(BF16)
        ys.append(jnp.dot(p, pw_ref[gi], preferred_element_type=F32))
    y = jnp.concatenate(ys, axis=1) * ps_ref[...]

    x3 = x + mod_ref[2:3, :] * y
    x3_ref[...] = x3
    f = _rms_mod(x3, ng2_ref[...], mod_ref[3:4, :], mod_ref[4:5, :])
    f_hi = f.astype(BF16)
    f_ref[...] = f_hi
    f_lo = (f - f_hi.astype(F32)).astype(BF16)
    z_hi = jnp.dot(f_hi, rw_ref[...], preferred_element_type=F32)
    logits = (z_hi[:, 0:LANES] + z_hi[:, LANES:2 * LANES]
              + jnp.dot(f_lo, rw_ref[:, 0:LANES], preferred_element_type=F32)) + rb_ref[...]
    lane = lax.broadcasted_iota(jnp.int32, logits.shape, 1)
    neg = jnp.float32(-jnp.inf)
    logits = jnp.where(lane < N_EXPERTS, logits, neg)
    v1 = jnp.max(logits, axis=-1, keepdims=True)
    i1 = jnp.min(jnp.where(logits == v1, lane, LANES), axis=-1, keepdims=True)
    rest = jnp.where(lane == i1, neg, logits)
    v2 = jnp.max(rest, axis=-1, keepdims=True)
    i2 = jnp.min(jnp.where(rest == v2, lane, LANES), axis=-1, keepdims=True)
    e2 = jnp.exp(v2 - v1)
    p1 = 1.0 / (1.0 + e2)
    p2 = e2 / (1.0 + e2)
    i2 = jnp.where(p2 != 0.0, i2, -1)
    route_ref[...] = jnp.where(lane == 0, i1.astype(F32), jnp.where(
        lane == 1, i2.astype(F32), jnp.where(lane == 2, p1, jnp.where(lane == 3, p2, 0.0))))
    routed = jnp.where((lane == i1) | (lane == i2), 1.0, 0.0)
    for h in range(tm // DISPATCH_T):
        cnt = jnp.sum(routed[h * DISPATCH_T:(h + 1) * DISPATCH_T, :], axis=0, keepdims=True)
        cnt_ref[h:h + 1, :] = cnt.astype(jnp.int32)


def _pool_router(x2, mod, ng1, ng2, pw, ps, rw, rb, tm):
    b, s, d = x2.shape
    nt = s // tm
    hb = tm // POOL_HALO
    rwh = rw.astype(BF16)
    rw2 = jnp.concatenate([rwh, (rw - rwh.astype(F32)).astype(BF16)], axis=1)
    tile = lambda w: pl.BlockSpec((None, tm, w), lambda bi, i: (bi, i, 0))
    return pl.pallas_call(
        functools.partial(_pool_kernel, tm=tm, seq=s),
        grid=(b, nt),
        in_specs=[
            tile(d),
            pl.BlockSpec((None, POOL_HALO, d), lambda bi, i: (bi, jnp.maximum(i * hb - 1, 0), 0)),
            pl.BlockSpec((None, POOL_HALO, d),
                         lambda bi, i: (bi, jnp.minimum((i + 1) * hb, nt * hb - 1), 0)),
            pl.BlockSpec((None, SUBLANES, d), lambda bi, i: (bi, 0, 0)),
            _const_spec((1, d)), _const_spec((1, d)), _const_spec(pw.shape), _const_spec((1, d)),
            _const_spec(rw2.shape), _const_spec(rb.shape),
        ],
        out_specs=[tile(d), tile(d), tile(LANES),
                   pl.BlockSpec((None, None, tm // DISPATCH_T, LANES), lambda bi, i: (bi, i, 0, 0))],
        out_shape=[jax.ShapeDtypeStruct((b, s, d), F32), jax.ShapeDtypeStruct((b, s, d), BF16),
                   jax.ShapeDtypeStruct((b, s, LANES), F32),
                   jax.ShapeDtypeStruct((b, nt, tm // DISPATCH_T, LANES), jnp.int32)],
        compiler_params=_cparams("parallel", "parallel"),
        name="pool_router",
    )(x2, x2, x2, mod, ng1, ng2, pw, ps, rw2, rb)


def _seg_copy(vmem_buf, slot, hbm, loc, dst, rows, sem, to_hbm):
    n = pl.multiple_of(rows, SEG_ALIGN)
    v = vmem_buf.at[slot, pl.ds(pl.multiple_of(loc, SEG_ALIGN), n)]
    h = hbm.at[pl.ds(pl.multiple_of(dst, SEG_ALIGN), n)]
    return pltpu.make_async_copy(v, h, sem) if to_hbm else pltpu.make_async_copy(h, v, sem)


def _start_segments(vmem_buf, slot, hbm, loc_s, gdst_s, nch_s, tile, sem, to_hbm):
    for e in range(N_EXPERTS):
        k = tile * N_EXPERTS + e
        rows = nch_s[k] * SEG_ALIGN

        @pl.when(rows > 0)
        def _(k=k, rows=rows):
            _seg_copy(vmem_buf, slot, hbm, loc_s[k], gdst_s[k], rows, sem, to_hbm).start()


def _wait_segments(vmem_buf, slot, hbm, rows, sem, to_hbm):
    @pl.when(rows > 0)
    def _():
        _seg_copy(vmem_buf, slot, hbm, 0, 0, rows, sem, to_hbm).wait()


def _dispatch_kernel(loc_s, gdst_s, nch_s, rows_s, tail_s, f_ref, route_ref, locrow_ref, tril_ref,
                     xs_hbm, d_ref, sbuf, zbuf, sem, zsem, *, g):
    i = pl.program_id(0)
    n_steps = pl.num_programs(0)
    t = DISPATCH_T
    base, other = (i % 2) * g, (1 - i % 2) * g
    row = lax.broadcasted_iota(jnp.int32, (DISPATCH_ROWS, t), 0).astype(F32)
    for sub in range(g):
        rows = slice(sub * t, (sub + 1) * t)
        route = route_ref[rows, :]
        lane = lax.broadcasted_iota(jnp.int32, route.shape, 1).astype(F32)
        first, second = lane == route[:, 0:1], lane == route[:, 1:2]
        onehot = jnp.where(first | second, 1.0, 0.0).astype(BF16)
        earlier = jnp.dot(tril_ref[...], onehot, preferred_element_type=F32)
        row_of = earlier + locrow_ref[sub]
        d_first = jnp.sum(jnp.where(first, row_of, 0.0), axis=1, keepdims=True)
        d_second = jnp.sum(jnp.where(second, row_of, 0.0), axis=1, keepdims=True)
        d_second = jnp.where(route[:, 1:2] < 0.0, -1.0, d_second)
        d = jnp.where(lane == 0.0, d_first, jnp.where(lane == 1.0, d_second, -1.0))
        d_ref[rows, :] = d
        dt = d.T
        perm = jnp.where((row == dt[0:1, :]) | (row == dt[1:2, :]), 1.0, 0.0)
        sbuf[base + sub] = jnp.dot(perm.astype(BF16), f_ref[rows, :], preferred_element_type=F32).astype(BF16)

    def copies(tile, slot, issue):
        if issue:
            _start_segments(sbuf, slot, xs_hbm, loc_s, gdst_s, nch_s, tile, sem.at[slot], True)
        else:
            _wait_segments(sbuf, slot, xs_hbm, rows_s[tile], sem.at[slot], True)

    for sub in range(g):
        copies(i * g + sub, base + sub, True)

    @pl.when(i > 0)
    def _():
        for sub in range(g):
            copies((i - 1) * g + sub, other + sub, False)

    @pl.when(i == n_steps - 1)
    def _():
        zbuf[...] = jnp.zeros(zbuf.shape, BF16)
        n_blk = xs_hbm.shape[0] // EXPERT_RB

        def gap_copy(e, j):
            dst = pl.multiple_of(tail_s[e] + j * SEG_ALIGN, SEG_ALIGN)
            return pltpu.make_async_copy(zbuf.at[pl.ds(0, SEG_ALIGN)], xs_hbm.at[pl.ds(dst, SEG_ALIGN)], zsem)

        def blk_copy(blk):
            dst = pl.multiple_of(blk * EXPERT_RB, EXPERT_RB)
            return pltpu.make_async_copy(zbuf, xs_hbm.at[pl.ds(dst, EXPERT_RB)], zsem)

        def fill(issue):
            def run(copy):
                copy.start() if issue else copy.wait()

            for e in range(N_EXPERTS):
                def gap_body(j, carry, e=e):
                    run(gap_copy(e, j))
                    return carry

                lax.fori_loop(0, tail_s[N_EXPERTS + e], gap_body, 0)

            def blk_body(blk, carry):
                run(blk_copy(blk))
                return carry

            lax.fori_loop(tail_s[2 * N_EXPERTS], n_blk, blk_body, 0)

        fill(True)
        for sub in range(g):
            copies(i * g + sub, base + sub, False)
        fill(False)


def _dispatch(f, route, loc, gdst, nch, rows, tail, n_blk, g):
    n, d = f.shape
    t = DISPATCH_T
    locrow = jnp.pad(loc.astype(F32), ((0, 0), (0, LANES - N_EXPERTS)))[:, None, :]
    tril = (jnp.arange(t)[:, None] > jnp.arange(t)[None, :]).astype(BF16)
    grid_spec = pltpu.PrefetchScalarGridSpec(
        num_scalar_prefetch=5,
        grid=(n // (g * t),),
        in_specs=[pl.BlockSpec((g * t, d), lambda i, *_: (i, 0)),
                  pl.BlockSpec((g * t, LANES), lambda i, *_: (i, 0)),
                  pl.BlockSpec((g, 1, LANES), lambda i, *_: (i, 0, 0)),
                  pl.BlockSpec((t, t), lambda i, *_: (0, 0))],
        out_specs=[pl.BlockSpec(memory_space=pl.ANY),
                   pl.BlockSpec((g * t, LANES), lambda i, *_: (i, 0))],
        scratch_shapes=[pltpu.VMEM((2 * g, DISPATCH_ROWS, d), BF16),
                        pltpu.VMEM((EXPERT_RB, d), BF16),
                        pltpu.SemaphoreType.DMA((2 * g,)),
                        pltpu.SemaphoreType.DMA(())],
    )
    return pl.pallas_call(
        functools.partial(_dispatch_kernel, g=g),
        grid_spec=grid_spec,
        out_shape=[jax.ShapeDtypeStruct((n_blk * EXPERT_RB, d), BF16),
                   jax.ShapeDtypeStruct((n, LANES), F32)],
        compiler_params=_cparams("arbitrary"),
        name="dispatch",
    )(loc.reshape(-1), gdst.reshape(-1), nch.reshape(-1), rows, tail, f, route, locrow, tril)


def _expert_kernel(exp_s, nvalid_s, xs_ref, wgu_ref, wd_ref, ys_ref):
    used = pl.program_id(0) < nvalid_s[0]

    @pl.when(used)
    def _():
        dff = wd_ref.shape[0]
        z = jnp.dot(xs_ref[...], wgu_ref[...], preferred_element_type=F32)
        hg, hu = z[:, 0:dff], z[:, dff:2 * dff]
        h = (hg * (1.0 / (1.0 + jnp.exp(-hg))) * hu).astype(BF16)
        ys_ref[...] = jnp.dot(h, wd_ref[...], preferred_element_type=F32).astype(BF16)

    @pl.when(jnp.logical_not(used))
    def _():
        ys_ref[...] = jnp.zeros(ys_ref.shape, BF16)


def _experts(xs, blk_exp, n_valid, wgu, wd):
    rows, d = xs.shape
    dff = wd.shape[1]
    grid_spec = pltpu.PrefetchScalarGridSpec(
        num_scalar_prefetch=2,
        grid=(rows // EXPERT_RB,),
        in_specs=[pl.BlockSpec((EXPERT_RB, d), lambda b, exp_s, nv: (b, 0)),
                  pl.BlockSpec((None, d, 2 * dff), lambda b, exp_s, nv: (exp_s[b], 0, 0)),
                  pl.BlockSpec((None, dff, d), lambda b, exp_s, nv: (exp_s[b], 0, 0))],
        out_specs=pl.BlockSpec((EXPERT_RB, d), lambda b, exp_s, nv: (b, 0)),
    )
    return pl.pallas_call(
        _expert_kernel,
        grid_spec=grid_spec,
        out_shape=jax.ShapeDtypeStruct((rows, d), BF16),
        compiler_params=_cparams("arbitrary"),
        name="experts",
    )(blk_exp, n_valid, xs, wgu, wd)


def _combine_kernel(loc_s, gdst_s, nch_s, rows_s, x_ref, route_ref, d_ref, mod_ref, fg_ref, ys_hbm,
                    o_ref, ybuf, sem, *, g):
    i = pl.program_id(0)
    n_steps = pl.num_programs(0)
    t = DISPATCH_T
    base, other = (i % 2) * g, (1 - i % 2) * g

    def copies(tile, slot, issue):
        if issue:
            _start_segments(ybuf, slot, ys_hbm, loc_s, gdst_s, nch_s, tile, sem.at[slot], False)
        else:
            _wait_segments(ybuf, slot, ys_hbm, rows_s[tile], sem.at[slot], False)

    @pl.when(i == 0)
    def _():
        ybuf[...] = jnp.zeros(ybuf.shape, BF16)
        for sub in range(g):
            copies(sub, base + sub, True)

    @pl.when(i + 1 < n_steps)
    def _():
        for sub in range(g):
            copies((i + 1) * g + sub, other + sub, True)

    for sub in range(g):
        copies(i * g + sub, base + sub, False)
    lane = lax.broadcasted_iota(jnp.int32, (t, DISPATCH_ROWS), 1).astype(F32)
    for sub in range(g):
        rows = slice(sub * t, (sub + 1) * t)
        route, d = route_ref[rows, :], d_ref[rows, :]
        gmat = jnp.where(lane == d[:, 0:1], route[:, 2:3], jnp.where(lane == d[:, 1:2], route[:, 3:4], 0.0))
        moe = jnp.dot(gmat.astype(BF16), ybuf[base + sub], preferred_element_type=F32)
        x4 = x_ref[rows, :] + mod_ref[5:6, :] * moe
        ms = jnp.mean(x4 * x4, axis=-1, keepdims=True)
        o_ref[rows, :] = x4 * lax.rsqrt(ms + EPS) * fg_ref[...]


def _combine(x3, route, dmap, ys, mod, fg, loc, gdst, nch, rows, steps_per_batch, g):
    n, d = x3.shape
    t = DISPATCH_T
    grid_spec = pltpu.PrefetchScalarGridSpec(
        num_scalar_prefetch=4,
        grid=(n // (g * t),),
        in_specs=[pl.BlockSpec((g * t, d), lambda i, *_: (i, 0)),
                  pl.BlockSpec((g * t, LANES), lambda i, *_: (i, 0)),
                  pl.BlockSpec((g * t, LANES), lambda i, *_: (i, 0)),
                  pl.BlockSpec((None, SUBLANES, d), lambda i, *_: (i // steps_per_batch, 0, 0)),
                  pl.BlockSpec((1, d), lambda i, *_: (0, 0)),
                  pl.BlockSpec(memory_space=pl.ANY)],
        out_specs=pl.BlockSpec((g * t, d), lambda i, *_: (i, 0)),
        scratch_shapes=[pltpu.VMEM((2 * g, DISPATCH_ROWS, d), BF16),
                        pltpu.SemaphoreType.DMA((2 * g,))],
    )
    return pl.pallas_call(
        functools.partial(_combine_kernel, g=g),
        grid_spec=grid_spec,
        out_shape=jax.ShapeDtypeStruct((n, d), F32),
        compiler_params=_cparams("arbitrary"),
        name="combine",
    )(loc.reshape(-1), gdst.reshape(-1), nch.reshape(-1), rows, x3, route, dmap, mod, fg, ys)


def _routing_tables(cnt, n_tokens):
    n_tiles = cnt.shape[0]
    i32 = lambda a: a.astype(jnp.int32)
    npad = (cnt + SEG_ALIGN - 1) // SEG_ALIGN * SEG_ALIGN
    loc = jnp.cumsum(npad, axis=1) - npad
    tot = jnp.sum(npad, axis=0)
    nblk = (tot + EXPERT_RB - 1) // EXPERT_RB
    ends = jnp.cumsum(nblk)
    base = (ends - nblk) * EXPERT_RB
    gdst = base[None, :] + jnp.cumsum(npad, axis=0) - npad
    nch = npad // SEG_ALIGN
    n_blk = (2 * n_tokens + (SEG_ALIGN - 1) * N_EXPERTS * n_tiles) // EXPERT_RB + N_EXPERTS
    blk_exp = jnp.minimum(jnp.searchsorted(ends, jnp.arange(n_blk), side="right"), N_EXPERTS - 1)
    tail = jnp.concatenate([base + tot, (nblk * EXPERT_RB - tot) // SEG_ALIGN, ends[-1:]])
    rows = jnp.sum(npad, axis=1)
    return n_blk, i32(loc), i32(gdst), i32(nch), i32(rows), i32(tail), i32(blk_exp), i32(ends[-1:])


def _rope_tables(n):
    t = jnp.arange(n)
    pos = jnp.stack([t // GRID_W, t % GRID_W], axis=1).astype(F32)
    freqs = ROPE_BASE ** (-jnp.arange(0, ROPE_AXIS_DIM, 2, dtype=F32) / ROPE_AXIS_DIM)
    ang = pos[:, :, None] * freqs
    ang = jnp.broadcast_to(ang[:, :, None, :], (n, 2, 2, ROPE_AXIS_DIM // 2))
    sign = jnp.array([-1.0, 1.0], F32)[None, None, :, None]
    cos = jnp.cos(ang).reshape(n, HEAD_DIM)
    sin = (jnp.sin(ang) * sign).reshape(n, HEAD_DIM)
    return jnp.tile(cos, (1, LANES // HEAD_DIM)), jnp.tile(sin, (1, LANES // HEAD_DIM))


def _tile_sizes(s):
    return min(512, s), min(512, s), min(2048, s)


def _pad_rows(m, rows):
    return jnp.pad(m, ((0, 0),) * (m.ndim - 2) + ((0, rows - m.shape[-2]), (0, 0)))


def kernel(x, c, ctx, c_ctx, w_mod, b_mod, norm_g, final_norm_g, w_mix_in, q_norm_g, k_norm_g, conv_w, w_mix_out, ffn_w_gate, ffn_w_up, ffn_w_down, pool_w, pool_scale, router_w, router_b, exp_w_gate, exp_w_up, exp_w_down):
    b, s, d = x.shape
    tm, tq, tk = _tile_sizes(s)

    rows = -(-(b + 1) // SUBLANES) * SUBLANES
    cvec = _pad_rows(jnp.concatenate([c, c_ctx[None, :]], axis=0), rows)
    mods = _adaln(cvec, w_mod, b_mod).reshape(w_mod.shape[0], rows, N_MOD, d)
    mod_lat = [_pad_rows(mods[l, :b], SUBLANES) for l in range(2)]
    mod_ctx0 = _pad_rows(mods[0, b], SUBLANES)

    w_in = w_mix_in[0].astype(BF16)
    cuts = (ATTN_W, ATTN_W + KV_W, ATTN_W + 2 * KV_W)
    wq, wkv, wc = w_in[:, :cuts[0]], w_in[:, cuts[0]:cuts[2]], w_in[:, cuts[2]:]
    head = jnp.arange(ATTN_W) // HEAD_DIM
    bd = ((head[:, None] == head[None, :]).astype(F32) / HEAD_DIM).astype(BF16)
    qg = jnp.tile(q_norm_g[0], N_HEADS)[None, :]
    kg = jnp.tile(k_norm_g[0], N_KV_HEADS)[None, :]
    cw = _pad_rows(conv_w[0], SUBLANES)
    cos, sin = _rope_tables(s)
    ng0a, ng0b = norm_g[0, 0][None, :], norm_g[0, 1][None, :]

    shift = (Q_SCALE * HEAD_DIM * SHIFT_MARGIN) * jnp.max(jnp.abs(q_norm_g[0])) * jnp.max(jnp.abs(k_norm_g[0]))
    qbias = jnp.where(jnp.arange(LANES - HEAD_DIM)[:, None] == 0, -shift, 0.0).astype(F32) * jnp.ones((1, tm), F32)
    q, kext, vext, conv = _l0_in(x, mod_lat[0], ng0a, wq, wkv, wc, bd, qg, kg, cw, cos, sin, qbias, tm)
    kext_c, vext_c = _ctx_kv(ctx, mod_ctx0, ng0a, wkv, bd, kg)
    attn = lax.cond(
        shift <= MAX_FIXED_SHIFT,
        lambda *a: _attention(*a, tq, tk, False),
        lambda *a: _attention(*a, tq, tk, True),
        q, kext_c, vext_c, kext, vext)

    w_out = w_mix_out[0].astype(BF16)
    n_chunks = 1
    dff = ffn_w_gate.shape[-1]
    split_cols = lambda w: w.astype(BF16).reshape(d, n_chunks, dff // n_chunks).transpose(1, 0, 2)
    wd0 = ffn_w_down[0].astype(BF16).reshape(n_chunks, dff // n_chunks, d)
    x2 = _l0_out_ffn(x, attn, conv, mod_lat[0], ng0b, w_out[:ATTN_W], w_out[ATTN_W:],
                     split_cols(ffn_w_gate[0]), split_cols(ffn_w_up[0]), wd0, tm)

    rw = jnp.pad(router_w[0], ((0, 0), (0, LANES - N_EXPERTS)))
    rb = jnp.pad(router_b[0], (0, LANES - N_EXPERTS))[None, :]
    x3, f2, route, cnt = _pool_router(x2, mod_lat[1], norm_g[1, 0][None, :], norm_g[1, 1][None, :],
                                      pool_w[0].astype(BF16), pool_scale[0][None, :], rw, rb, tm)
    n = b * s
    n_blk, loc, gdst, nch, rows, tail, blk_exp, n_valid = _routing_tables(
        cnt.reshape(n // DISPATCH_T, LANES)[:, :N_EXPERTS], n)
    route = route.reshape(n, LANES)
    tiles_per_batch = s // DISPATCH_T
    g = max(k for k in (4, 2, 1) if tiles_per_batch % k == 0)
    xs, dmap = _dispatch(f2.reshape(n, d), route, loc, gdst, nch, rows, tail, n_blk, g)
    wgu = jnp.concatenate([exp_w_gate[0].astype(BF16), exp_w_up[0].astype(BF16)], axis=-1)
    ys = _experts(xs, blk_exp, n_valid, wgu, exp_w_down[0].astype(BF16))
    out = _combine(x3.reshape(n, d), route, dmap, ys, mod_lat[1], final_norm_g[None, :],
                   loc, gdst, nch, rows, tiles_per_batch // g, g)
    return out.reshape(b, s, d)
```

```python
import functools
import math

import jax
import jax.numpy as jnp
from jax import lax
from jax.experimental import pallas as pl
from jax.experimental.pallas import tpu as pltpu

F32 = jnp.float32
BF16 = jnp.bfloat16

D_MODEL = 1024
GRID_W = 64
N_HEADS = 8
N_KV_HEADS = 2
HEAD_DIM = 64
ATTN_W = N_HEADS * HEAD_DIM
KV_W = N_KV_HEADS * HEAD_DIM
CONV_W = D_MODEL - ATTN_W
ROPE_AXIS_DIM = HEAD_DIM // 2
ROPE_BASE = 10000.0
POOL_WINDOWS = (2, 4, 8, 16)
POOL_GROUP_W = D_MODEL // len(POOL_WINDOWS)
POOL_HALO = 8
N_EXPERTS = 8
DISPATCH_T = 256
SEG_ALIGN = 16
DISPATCH_ROWS = 2 * DISPATCH_T + N_EXPERTS * SEG_ALIGN
EXPERT_RB = 512
N_MOD = 6
EPS = 1e-6

LANES = 128
SUBLANES = 8
VMEM_LIMIT = 56 * 1024 * 1024

Q_SCALE = (HEAD_DIM ** -0.5) * math.log2(math.e)
SHIFT_MARGIN = 1.01
MAX_FIXED_SHIFT = 48.0


def _cparams(*sem):
    return pltpu.CompilerParams(dimension_semantics=sem, vmem_limit_bytes=VMEM_LIMIT)


def _const_spec(shape):
    nd = len(shape)
    return pl.BlockSpec(shape, lambda *_: (0,) * nd, pipeline_mode=pl.Buffered(1))


def _rms_mod(x, g, shift, scale):
    ms = jnp.mean(x * x, axis=-1, keepdims=True)
    return x * lax.rsqrt(ms + EPS) * (g * (1.0 + scale)) + shift


def _head_rms(z, bd, g):
    ms = jnp.dot((z * z).astype(BF16), bd, preferred_element_type=F32)
    return z * lax.rsqrt(ms + EPS) * g


def _rope128(x, cos, sin_signed):
    lane = lax.broadcasted_iota(jnp.int32, x.shape, 1)
    partner = jnp.where((lane & 16) == 0, pltpu.roll(x, LANES - 16, 1), pltpu.roll(x, 16, 1))
    return x * cos + partner * sin_signed


def _adaln_kernel(c_ref, w_ref, b_ref, o_ref):
    c = c_ref[...]
    s = c * (1.0 / (1.0 + jnp.exp(-c)))
    o_ref[...] = jnp.dot(s, w_ref[...], preferred_element_type=F32,
                         precision=lax.Precision.HIGHEST) + b_ref[...]


def _adaln(cvec, w_mod, b_mod):
    depth, d, n = w_mod.shape
    rows = cvec.shape[0]
    tn = 1536
    return pl.pallas_call(
        _adaln_kernel,
        grid=(depth, n // tn),
        in_specs=[pl.BlockSpec((rows, d), lambda l, j: (0, 0)),
                  pl.BlockSpec((None, d, tn), lambda l, j: (l, 0, j)),
                  pl.BlockSpec((None, 1, tn), lambda l, j: (l, 0, j))],
        out_specs=pl.BlockSpec((None, rows, tn), lambda l, j: (l, 0, j)),
        out_shape=jax.ShapeDtypeStruct((depth, rows, n), F32),
        compiler_params=_cparams("parallel", "parallel"),
        name="adaln",
    )(cvec, w_mod, b_mod.reshape(depth, 1, n))


def _kv_outputs(zk, zv, kext_ref, vext_ref):
    rows = zk.shape[0]
    lane = lax.broadcasted_iota(jnp.int32, zk.shape, 1)
    pad = jnp.where(lane == HEAD_DIM, 1.0, 0.0)
    zk_swapped = pltpu.roll(zk, HEAD_DIM, 1)
    vt = zv.T
    sub = lax.broadcasted_iota(jnp.int32, (LANES - HEAD_DIM, rows), 0)
    ones_row = jnp.where(sub == 0, 1.0, 0.0)
    for g in range(N_KV_HEADS):
        kg = zk if g == 0 else zk_swapped
        kext_ref[g] = jnp.where(lane < HEAD_DIM, kg, pad).astype(BF16)
        vext_ref[g] = jnp.concatenate([vt[g * HEAD_DIM:(g + 1) * HEAD_DIM, :], ones_row], axis=0).astype(BF16)


def _l0_in_kernel(x_ref, xp_ref, xn_ref, mod_ref, ng_ref, wq_ref, wkv_ref, wc_ref,
                  bd_ref, qg_ref, kg_ref, cw_ref, cos_ref, sin_ref, qb_ref,
                  q_ref, kext_ref, vext_ref, conv_ref, *, tm):
    i = pl.program_id(1)
    ni = pl.num_programs(1)
    g = ng_ref[...]
    shift, scale = mod_ref[0:1, :], mod_ref[1:2, :]
    a = _rms_mod(x_ref[...], g, shift, scale).astype(BF16)
    cos, sin = cos_ref[...], sin_ref[...]

    zq = jnp.dot(a, wq_ref[...], preferred_element_type=F32)
    qn = _head_rms(zq, bd_ref[...], qg_ref[...])
    qbias = qb_ref[...]
    for c in range(ATTN_W // LANES):
        rt = (_rope128(qn[:, c * LANES:(c + 1) * LANES], cos, sin) * Q_SCALE).T
        for hh in range(2):
            q_ref[2 * c + hh] = jnp.concatenate(
                [rt[hh * HEAD_DIM:(hh + 1) * HEAD_DIM, :], qbias], axis=0).astype(BF16)

    zkv = jnp.dot(a, wkv_ref[...], preferred_element_type=F32)
    kn = _rope128(_head_rms(zkv[:, 0:KV_W], bd_ref[0:KV_W, 0:KV_W], kg_ref[...]), cos, sin)
    _kv_outputs(kn, zkv[:, KV_W:2 * KV_W], kext_ref, vext_ref)

    zc = jnp.dot(a, wc_ref[...], preferred_element_type=F32)
    gb, w = zc[:, 0:CONV_W], zc[:, CONV_W:2 * CONV_W] * zc[:, 2 * CONV_W:3 * CONV_W]
    halo = jnp.concatenate([xp_ref[...], xn_ref[...]], axis=0)
    ah = _rms_mod(halo, g, shift, scale).astype(BF16)
    zh = jnp.dot(ah, wc_ref[:, CONV_W:3 * CONV_W], preferred_element_type=F32)
    wh = zh[:, 0:CONV_W] * zh[:, CONV_W:2 * CONV_W]
    w_before = wh[SUBLANES - 1:SUBLANES, :] * (i > 0).astype(F32)
    w_after = wh[SUBLANES:SUBLANES + 1, :] * (i < ni - 1).astype(F32)
    row = lax.broadcasted_iota(jnp.int32, w.shape, 0)
    w_prev = jnp.where(row == 0, w_before, pltpu.roll(w, 1, 0))
    w_next = jnp.where(row == tm - 1, w_after, pltpu.roll(w, tm - 1, 0))
    y = gb * (w_prev * cw_ref[0:1, :] + w * cw_ref[1:2, :] + w_next * cw_ref[2:3, :])
    conv_ref[...] = y.astype(BF16)


def _l0_in(x, mod, ng, wq, wkv, wc, bd, qg, kg, cw, cos, sin, qbias, tm):
    b, s, d = x.shape
    nt = s // tm
    hb = tm // SUBLANES
    return pl.pallas_call(
        functools.partial(_l0_in_kernel, tm=tm),
        grid=(b, nt),
        in_specs=[
            pl.BlockSpec((None, tm, d), lambda bi, i: (bi, i, 0)),
            pl.BlockSpec((None, SUBLANES, d), lambda bi, i: (bi, jnp.maximum(i * hb - 1, 0), 0)),
            pl.BlockSpec((None, SUBLANES, d),
                         lambda bi, i: (bi, jnp.minimum((i + 1) * hb, nt * hb - 1), 0)),
            pl.BlockSpec((None, SUBLANES, d), lambda bi, i: (bi, 0, 0)),
            _const_spec((1, d)),
            _const_spec(wq.shape), _const_spec(wkv.shape),
            _const_spec(wc.shape), _const_spec(bd.shape),
            _const_spec(qg.shape), _const_spec(kg.shape), _const_spec(cw.shape),
            pl.BlockSpec((tm, LANES), lambda bi, i: (i, 0)),
            pl.BlockSpec((tm, LANES), lambda bi, i: (i, 0)),
            _const_spec(qbias.shape),
        ],
        out_specs=[
            pl.BlockSpec((None, N_HEADS, LANES, tm), lambda bi, i: (bi, 0, 0, i)),
            pl.BlockSpec((None, N_KV_HEADS, tm, LANES), lambda bi, i: (bi, 0, i, 0)),
            pl.BlockSpec((None, N_KV_HEADS, LANES, tm), lambda bi, i: (bi, 0, 0, i)),
            pl.BlockSpec((None, tm, CONV_W), lambda bi, i: (bi, i, 0)),
        ],
        out_shape=[
            jax.ShapeDtypeStruct((b, N_HEADS, LANES, s), BF16),
            jax.ShapeDtypeStruct((b, N_KV_HEADS, s, LANES), BF16),
            jax.ShapeDtypeStruct((b, N_KV_HEADS, LANES, s), BF16),
            jax.ShapeDtypeStruct((b, s, CONV_W), BF16),
        ],
        compiler_params=_cparams("parallel", "parallel"),
        name="l0_in",
    )(x, x, x, mod, ng, wq, wkv, wc, bd, qg, kg, cw, cos, sin, qbias)


def _ctx_kv_kernel(x_ref, mod_ref, ng_ref, wkv_ref, bd_ref, kg_ref, kext_ref, vext_ref):
    a = _rms_mod(x_ref[...], ng_ref[...], mod_ref[0:1, :], mod_ref[1:2, :]).astype(BF16)
    zkv = jnp.dot(a, wkv_ref[...], preferred_element_type=F32)
    kn = _head_rms(zkv[:, 0:KV_W], bd_ref[0:KV_W, 0:KV_W], kg_ref[...])
    _kv_outputs(kn, zkv[:, KV_W:2 * KV_W], kext_ref, vext_ref)


def _ctx_kv(ctx, mod, ng, wkv, bd, kg):
    b, nc, d = ctx.shape
    return pl.pallas_call(
        _ctx_kv_kernel,
        grid=(b,),
        in_specs=[
            pl.BlockSpec((None, nc, d), lambda bi: (bi, 0, 0)),
            _const_spec(mod.shape), _const_spec((1, d)),
            _const_spec(wkv.shape), _const_spec(bd.shape), _const_spec(kg.shape),
        ],
        out_specs=[
            pl.BlockSpec((None, N_KV_HEADS, nc, LANES), lambda bi: (bi, 0, 0, 0)),
            pl.BlockSpec((None, N_KV_HEADS, LANES, nc), lambda bi: (bi, 0, 0, 0)),
        ],
        out_shape=[
            jax.ShapeDtypeStruct((b, N_KV_HEADS, nc, LANES), BF16),
            jax.ShapeDtypeStruct((b, N_KV_HEADS, LANES, nc), BF16),
        ],
        compiler_params=_cparams("parallel"),
        name="ctx_kv",
    )(ctx, mod, ng, wkv, bd, kg)


def _attn_kernel(q_ref, kc_ref, vc_ref, k_ref, v_ref, o_ref, qs_ref, acc_ref, *m_scratch,
                 tq, tk, nk, online):
    n_h = q_ref.shape[0]
    for h in range(n_h):
        qs_ref[:, h * tq:(h + 1) * tq] = q_ref[h]

    if online:
        m_ref, = m_scratch
        m_ref[...] = jnp.full(m_ref.shape, -jnp.inf, F32)
        acc_ref[...] = jnp.zeros(acc_ref.shape, F32)

    def step(kx, vt, first):
        s = jnp.dot(kx, qs_ref[...], preferred_element_type=F32)
        if online:
            m_prev = m_ref[...]
            m_new = jnp.maximum(m_prev, jnp.max(s, axis=0, keepdims=True))
            p = jnp.exp2(s - m_new[0:1, :]).astype(BF16)
            acc_ref[...] = (jnp.exp2(m_prev - m_new)[0:1, :] * acc_ref[...]
                            + jnp.dot(vt, p, preferred_element_type=F32))
            m_ref[...] = m_new
        else:
            pv = jnp.dot(vt, jnp.exp2(s).astype(BF16), preferred_element_type=F32)
            if first:
                acc_ref[...] = pv
            else:
                acc_ref[...] += pv

    step(kc_ref[...], vc_ref[...], True)

    def body(c, carry):
        off = pl.multiple_of(c * tk, tk)
        step(k_ref[pl.ds(off, tk), :], v_ref[:, pl.ds(off, tk)], False)
        return carry

    lax.fori_loop(0, nk, body, 0, unroll=not online)

    acc = acc_ref[...]
    r = acc[0:HEAD_DIM, :] / acc[HEAD_DIM:HEAD_DIM + 1, :]
    for p in range(n_h // 2):
        pair = jnp.concatenate([r[:, (2 * p) * tq:(2 * p + 1) * tq],
                                r[:, (2 * p + 1) * tq:(2 * p + 2) * tq]], axis=0)
        o_ref[:, p * LANES:(p + 1) * LANES] = pair.T.astype(BF16)


def _attention(q, kext_c, vext_c, kext, vext, tq, tk, online):
    b, _, _, s = q.shape
    nc = kext_c.shape[-2]
    n_h = N_HEADS // N_KV_HEADS
    cols = n_h * tq
    scratch = [pltpu.VMEM((LANES, cols), BF16), pltpu.VMEM((LANES, cols), F32)]
    if online:
        scratch.append(pltpu.VMEM((SUBLANES, cols), F32))
    return pl.pallas_call(
        functools.partial(_attn_kernel, tq=tq, tk=tk, nk=s // tk, online=online),
        grid=(b, N_KV_HEADS, s // tq),
        in_specs=[
            pl.BlockSpec((None, n_h, LANES, tq), lambda bi, g, i: (bi, g, 0, i)),
            pl.BlockSpec((None, None, nc, LANES), lambda bi, g, i: (bi, g, 0, 0)),
            pl.BlockSpec((None, None, LANES, nc), lambda bi, g, i: (bi, g, 0, 0)),
            pl.BlockSpec((None, None, s, LANES), lambda bi, g, i: (bi, g, 0, 0)),
            pl.BlockSpec((None, None, LANES, s), lambda bi, g, i: (bi, g, 0, 0)),
        ],
        out_specs=pl.BlockSpec((None, tq, n_h * HEAD_DIM), lambda bi, g, i: (bi, i, g)),
        out_shape=jax.ShapeDtypeStruct((b, s, ATTN_W), BF16),
        scratch_shapes=scratch,
        compiler_params=_cparams("parallel", "parallel", "parallel"),
        name="attention_online" if online else "attention",
    )(q, kext_c, vext_c, kext, vext)


def _l0_out_ffn_kernel(x_ref, attn_ref, conv_ref, mod_ref, ng_ref, wa_ref, wc_ref, wg_ref, wu_ref, wd_ref,
                       o_ref, *, n_chunks):
    y = (jnp.dot(attn_ref[...], wa_ref[...], preferred_element_type=F32)
         + jnp.dot(conv_ref[...], wc_ref[...], preferred_element_type=F32))
    x1 = x_ref[...] + mod_ref[2:3, :] * y
    f = _rms_mod(x1, ng_ref[...], mod_ref[3:4, :], mod_ref[4:5, :]).astype(BF16)
    acc = None
    for c in range(n_chunks):
        hg = jnp.dot(f, wg_ref[c], preferred_element_type=F32)
        hu = jnp.dot(f, wu_ref[c], preferred_element_type=F32)
        h = (hg * (1.0 / (1.0 + jnp.exp(-hg))) * hu).astype(BF16)
        part = jnp.dot(h, wd_ref[c], preferred_element_type=F32)
        acc = part if acc is None else acc + part
    o_ref[...] = x1 + mod_ref[5:6, :] * acc


def _l0_out_ffn(x, attn, conv, mod, ng, wa, wc, wg, wu, wd, tm):
    b, s, d = x.shape
    tile = lambda w: pl.BlockSpec((None, tm, w), lambda bi, i: (bi, i, 0))
    return pl.pallas_call(
        functools.partial(_l0_out_ffn_kernel, n_chunks=wg.shape[0]),
        grid=(b, s // tm),
        in_specs=[tile(d), tile(ATTN_W), tile(CONV_W),
                  pl.BlockSpec((None, SUBLANES, d), lambda bi, i: (bi, 0, 0)),
                  _const_spec((1, d)), _const_spec(wa.shape), _const_spec(wc.shape),
                  _const_spec(wg.shape), _const_spec(wu.shape), _const_spec(wd.shape)],
        out_specs=tile(d),
        out_shape=jax.ShapeDtypeStruct((b, s, d), F32),
        compiler_params=_cparams("parallel", "parallel"),
        name="l0_out_ffn",
    )(x, attn, conv, mod, ng, wa, wc, wg, wu, wd)


def _pool_kernel(x_ref, xp_ref, xn_ref, mod_ref, ng1_ref, ng2_ref, pw_ref, ps_ref, rw_ref, rb_ref,
                 x3_ref, f_ref, route_ref, cnt_ref, *, tm, seq):
    i = pl.program_id(1)
    ni = pl.num_programs(1)
    g = ng1_ref[...]
    shift, scale = mod_ref[0:1, :], mod_ref[1:2, :]
    x = x_ref[...]
    a = _rms_mod(x, g, shift, scale)
    a_before = _rms_mod(xp_ref[...], g, shift, scale) * (i > 0).astype(F32)
    a_after = _rms_mod(xn_ref[...], g, shift, scale) * (i < ni - 1).astype(F32)
    ext = jnp.concatenate([a_before, a, a_after], axis=0)

    t = i * tm + lax.broadcasted_iota(jnp.int32, (tm, 1), 0)
    ys = []
    for gi, w in enumerate(POOL_WINDOWS):
        sl = slice(gi * POOL_GROUP_W, (gi + 1) * POOL_GROUP_W)
        e = ext[:, sl]
        acc, span = e, 1
        while span < w:
            acc = acc + pltpu.roll(acc, span, 0)
            span *= 2
        win = acc[POOL_HALO + w // 2 - 1:POOL_HALO + w // 2 - 1 + tm, :]
        cnt = jnp.minimum(t + w // 2, seq) - jnp.maximum(t - w // 2, 0)
        p = (win / cnt.astype(F32) - a[:, sl]).astype(BF16)
        ys.append(jnp.dot(p, pw_ref[gi], preferred_element_type=F32))
    y = jnp.concatenate(ys, axis=1) * ps_ref[...]

    x3 = x + mod_ref[2:3, :] * y
    x3_ref[...] = x3
    f = _rms_mod(x3, ng2_ref[...], mod_ref[3:4, :], mod_ref[4:5, :])
    f_hi = f.astype(BF16)
    f_ref[...] = f_hi
    f_lo = (f - f_hi.astype(F32)).astype(BF16)
    z_hi = jnp.dot(f_hi, rw_ref[...], preferred_element_type=F32)
    logits = (z_hi[:, 0:LANES] + z_hi[:, LANES:2 * LANES]
              + jnp.dot(f_lo, rw_ref[:, 0:LANES], preferred_element_type=F32)) + rb_ref[...]
    lane = lax.broadcasted_iota(jnp.int32, logits.shape, 1)
    neg = jnp.float32(-jnp.inf)
    logits = jnp.where(lane < N_EXPERTS, logits, neg)
    v1 = jnp.max(logits, axis=-1, keepdims=True)
    i1 = jnp.min(jnp.where(logits == v1, lane, LANES), axis=-1, keepdims=True)
    rest = jnp.where(lane == i1, neg, logits)
    v2 = jnp.max(rest, axis=-1, keepdims=True)
    i2 = jnp.min(jnp.where(rest == v2, lane, LANES), axis=-1, keepdims=True)
    e2 = jnp.exp(v2 - v1)
    p1 = 1.0 / (1.0 + e2)
    p2 = e2 / (1.0 + e2)
    i2 = jnp.where(p2 != 0.0, i2, -1)
    route_ref[...] = jnp.where(lane == 0, i1.astype(F32), jnp.where(
        lane == 1, i2.astype(F32), jnp.where(lane == 2, p1, jnp.where(lane == 3, p2, 0.0))))
    routed = jnp.where((lane == i1) | (lane == i2), 1.0, 0.0)
    for h in range(tm // DISPATCH_T):
        cnt = jnp.sum(routed[h * DISPATCH_T:(h + 1) * DISPATCH_T, :], axis=0, keepdims=True)
        cnt_ref[h:h + 1, :] = cnt.astype(jnp.int32)


def _pool_router(x2, mod, ng1, ng2, pw, ps, rw, rb, tm):
    b, s, d = x2.shape
    nt = s // tm
    hb = tm // POOL_HALO
    rwh = rw.astype(BF16)
    rw2 = jnp.concatenate([rwh, (rw - rwh.astype(F32)).astype(BF16)], axis=1)
    tile = lambda w: pl.BlockSpec((None, tm, w), lambda bi, i: (bi, i, 0))
    return pl.pallas_call(
        functools.partial(_pool_kernel, tm=tm, seq=s),
        grid=(b, nt),
        in_specs=[
            tile(d),
            pl.BlockSpec((None, POOL_HALO, d), lambda bi, i: (bi, jnp.maximum(i * hb - 1, 0), 0)),
            pl.BlockSpec((None, POOL_HALO, d),
                         lambda bi, i: (bi, jnp.minimum((i + 1) * hb, nt * hb - 1), 0)),
            pl.BlockSpec((None, SUBLANES, d), lambda bi, i: (bi, 0, 0)),
            _const_spec((1, d)), _const_spec((1, d)), _const_spec(pw.shape), _const_spec((1, d)),
            _const_spec(rw2.shape), _const_spec(rb.shape),
        ],
        out_specs=[tile(d), tile(d), tile(LANES),
                   pl.BlockSpec((None, None, tm // DISPATCH_T, LANES), lambda bi, i: (bi, i, 0, 0))],
        out_shape=[jax.ShapeDtypeStruct((b, s, d), F32), jax.ShapeDtypeStruct((b, s, d), BF16),
                   jax.ShapeDtypeStruct((b, s, LANES), F32),
                   jax.ShapeDtypeStruct((b, nt, tm // DISPATCH_T, LANES), jnp.int32)],
        compiler_params=_cparams("parallel", "parallel"),
        name="pool_router",
    )(x2, x2, x2, mod, ng1, ng2, pw, ps, rw2, rb)


def _seg_copy(vmem_buf, slot, hbm, loc, dst, rows, sem, to_hbm):
    n = pl.multiple_of(rows, SEG_ALIGN)
    v = vmem_buf.at[slot, pl.ds(pl.multiple_of(loc, SEG_ALIGN), n)]
    h = hbm.at[pl.ds(pl.multiple_of(dst, SEG_ALIGN), n)]
    return pltpu.make_async_copy(v, h, sem) if to_hbm else pltpu.make_async_copy(h, v, sem)


def _start_segments(vmem_buf, slot, hbm, loc_s, gdst_s, nch_s, tile, sem, to_hbm):
    for e in range(N_EXPERTS):
        k = tile * N_EXPERTS + e
        rows = nch_s[k] * SEG_ALIGN

        @pl.when(rows > 0)
        def _(k=k, rows=rows):
            _seg_copy(vmem_buf, slot, hbm, loc_s[k], gdst_s[k], rows, sem, to_hbm).start()


def _wait_segments(vmem_buf, slot, hbm, rows, sem, to_hbm):
    @pl.when(rows > 0)
    def _():
        _seg_copy(vmem_buf, slot, hbm, 0, 0, rows, sem, to_hbm).wait()


def _dispatch_kernel(loc_s, gdst_s, nch_s, rows_s, tail_s, f_ref, route_ref, locrow_ref, tril_ref,
                     xs_hbm, d_ref, sbuf, zbuf, sem, zsem, *, g):
    i = pl.program_id(0)
    n_steps = pl.num_programs(0)
    t = DISPATCH_T
    base, other = (i % 2) * g, (1 - i % 2) * g
    row = lax.broadcasted_iota(jnp.int32, (DISPATCH_ROWS, t), 0).astype(F32)
    for sub in range(g):
        rows = slice(sub * t, (sub + 1) * t)
        route = route_ref[rows, :]
        lane = lax.broadcasted_iota(jnp.int32, route.shape, 1).astype(F32)
        first, second = lane == route[:, 0:1], lane == route[:, 1:2]
        onehot = jnp.where(first | second, 1.0, 0.0).astype(BF16)
        earlier = jnp.dot(tril_ref[...], onehot, preferred_element_type=F32)
        row_of = earlier + locrow_ref[sub]
        d_first = jnp.sum(jnp.where(first, row_of, 0.0), axis=1, keepdims=True)
        d_second = jnp.sum(jnp.where(second, row_of, 0.0), axis=1, keepdims=True)
        d_second = jnp.where(route[:, 1:2] < 0.0, -1.0, d_second)
        d = jnp.where(lane == 0.0, d_first, jnp.where(lane == 1.0, d_second, -1.0))
        d_ref[rows, :] = d
        dt = d.T
        perm = jnp.where((row == dt[0:1, :]) | (row == dt[1:2, :]), 1.0, 0.0)
        sbuf[base + sub] = jnp.dot(perm.astype(BF16), f_ref[rows, :], preferred_element_type=F32).astype(BF16)

    def copies(tile, slot, issue):
        if issue:
            _start_segments(sbuf, slot, xs_hbm, loc_s, gdst_s, nch_s, tile, sem.at[slot], True)
        else:
            _wait_segments(sbuf, slot, xs_hbm, rows_s[tile], sem.at[slot], True)

    for sub in range(g):
        copies(i * g + sub, base + sub, True)

    @pl.when(i > 0)
    def _():
        for sub in range(g):
            copies((i - 1) * g + sub, other + sub, False)

    @pl.when(i == n_steps - 1)
    def _():
        zbuf[...] = jnp.zeros(zbuf.shape, BF16)
        n_blk = xs_hbm.shape[0] // EXPERT_RB

        def gap_copy(e, j):
            dst = pl.multiple_of(tail_s[e] + j * SEG_ALIGN, SEG_ALIGN)
            return pltpu.make_async_copy(zbuf.at[pl.ds(0, SEG_ALIGN)], xs_hbm.at[pl.ds(dst, SEG_ALIGN)], zsem)

        def blk_copy(blk):
            dst = pl.multiple_of(blk * EXPERT_RB, EXPERT_RB)
            return pltpu.make_async_copy(zbuf, xs_hbm.at[pl.ds(dst, EXPERT_RB)], zsem)

        def fill(issue):
            def run(copy):
                copy.start() if issue else copy.wait()

            for e in range(N_EXPERTS):
                def gap_body(j, carry, e=e):
                    run(gap_copy(e, j))
                    return carry

                lax.fori_loop(0, tail_s[N_EXPERTS + e], gap_body, 0)

            def blk_body(blk, carry):
                run(blk_copy(blk))
                return carry

            lax.fori_loop(tail_s[2 * N_EXPERTS], n_blk, blk_body, 0)

        fill(True)
        for sub in range(g):
            copies(i * g + sub, base + sub, False)
        fill(False)


def _dispatch(f, route, loc, gdst, nch, rows, tail, n_blk, g):
    n, d = f.shape
    t = DISPATCH_T
    locrow = jnp.pad(loc.astype(F32), ((0, 0), (0, LANES - N_EXPERTS)))[:, None, :]
    tril = (jnp.arange(t)[:, None] > jnp.arange(t)[None, :]).astype(BF16)
    grid_spec = pltpu.PrefetchScalarGridSpec(
        num_scalar_prefetch=5,
        grid=(n // (g * t),),
        in_specs=[pl.BlockSpec((g * t, d), lambda i, *_: (i, 0)),
                  pl.BlockSpec((g * t, LANES), lambda i, *_: (i, 0)),
                  pl.BlockSpec((g, 1, LANES), lambda i, *_: (i, 0, 0)),
                  pl.BlockSpec((t, t), lambda i, *_: (0, 0))],
        out_specs=[pl.BlockSpec(memory_space=pl.ANY),
                   pl.BlockSpec((g * t, LANES), lambda i, *_: (i, 0))],
        scratch_shapes=[pltpu.VMEM((2 * g, DISPATCH_ROWS, d), BF16),
                        pltpu.VMEM((EXPERT_RB, d), BF16),
                        pltpu.SemaphoreType.DMA((2 * g,)),
                        pltpu.SemaphoreType.DMA(())],
    )
    return pl.pallas_call(
        functools.partial(_dispatch_kernel, g=g),
        grid_spec=grid_spec,
        out_shape=[jax.ShapeDtypeStruct((n_blk * EXPERT_RB, d), BF16),
                   jax.ShapeDtypeStruct((n, LANES), F32)],
        compiler_params=_cparams("arbitrary"),
        name="dispatch",
    )(loc.reshape(-1), gdst.reshape(-1), nch.reshape(-1), rows, tail, f, route, locrow, tril)


def _expert_kernel(exp_s, nvalid_s, xs_ref, wgu_ref, wd_ref, ys_ref):
    used = pl.program_id(0) < nvalid_s[0]

    @pl.when(used)
    def _():
        dff = wd_ref.shape[0]
        z = jnp.dot(xs_ref[...], wgu_ref[...], preferred_element_type=F32)
        hg, hu = z[:, 0:dff], z[:, dff:2 * dff]
        h = (hg * (1.0 / (1.0 + jnp.exp(-hg))) * hu).astype(BF16)
        ys_ref[...] = jnp.dot(h, wd_ref[...], preferred_element_type=F32).astype(BF16)

    @pl.when(jnp.logical_not(used))
    def _():
        ys_ref[...] = jnp.zeros(ys_ref.shape, BF16)


def _experts(xs, blk_exp, n_valid, wgu, wd):
    rows, d = xs.shape
    dff = wd.shape[1]
    grid_spec = pltpu.PrefetchScalarGridSpec(
        num_scalar_prefetch=2,
        grid=(rows // EXPERT_RB,),
        in_specs=[pl.BlockSpec((EXPERT_RB, d), lambda b, exp_s, nv: (b, 0)),
                  pl.BlockSpec((None, d, 2 * dff), lambda b, exp_s, nv: (exp_s[b], 0, 0)),
                  pl.BlockSpec((None, dff, d), lambda b, exp_s, nv: (exp_s[b], 0, 0))],
        out_specs=pl.BlockSpec((EXPERT_RB, d), lambda b, exp_s, nv: (b, 0)),
    )
    return pl.pallas_call(
        _expert_kernel,
        grid_spec=grid_spec,
        out_shape=jax.ShapeDtypeStruct((rows, d), BF16),
        compiler_params=_cparams("arbitrary"),
        name="experts",
    )(blk_exp, n_valid, xs, wgu, wd)


def _combine_kernel(loc_s, gdst_s, nch_s, rows_s, x_ref, route_ref, d_ref, mod_ref, fg_ref, ys_hbm,
                    o_ref, ybuf, sem, *, g):
    i = pl.program_id(0)
    n_steps = pl.num_programs(0)
    t = DISPATCH_T
    base, other = (i % 2) * g, (1 - i % 2) * g

    def copies(tile, slot, issue):
        if issue:
            _start_segments(ybuf, slot, ys_hbm, loc_s, gdst_s, nch_s, tile, sem.at[slot], False)
        else:
            _wait_segments(ybuf, slot, ys_hbm, rows_s[tile], sem.at[slot], False)

    @pl.when(i == 0)
    def _():
        ybuf[...] = jnp.zeros(ybuf.shape, BF16)
        for sub in range(g):
            copies(sub, base + sub, True)

    @pl.when(i + 1 < n_steps)
    def _():
        for sub in range(g):
            copies((i + 1) * g + sub, other + sub, True)

    for sub in range(g):
        copies(i * g + sub, base + sub, False)
    lane = lax.broadcasted_iota(jnp.int32, (t, DISPATCH_ROWS), 1).astype(F32)
    for sub in range(g):
        rows = slice(sub * t, (sub + 1) * t)
        route, d = route_ref[rows, :], d_ref[rows, :]
        gmat = jnp.where(lane == d[:, 0:1], route[:, 2:3], jnp.where(lane == d[:, 1:2], route[:, 3:4], 0.0))
        moe = jnp.dot(gmat.astype(BF16), ybuf[base + sub], preferred_element_type=F32)
        x4 = x_ref[rows, :] + mod_ref[5:6, :] * moe
        ms = jnp.mean(x4 * x4, axis=-1, keepdims=True)
        o_ref[rows, :] = x4 * lax.rsqrt(ms + EPS) * fg_ref[...]


def _combine(x3, route, dmap, ys, mod, fg, loc, gdst, nch, rows, steps_per_batch, g):
    n, d = x3.shape
    t = DISPATCH_T
    grid_spec = pltpu.PrefetchScalarGridSpec(
        num_scalar_prefetch=4,
        grid=(n // (g * t),),
        in_specs=[pl.BlockSpec((g * t, d), lambda i, *_: (i, 0)),
                  pl.BlockSpec((g * t, LANES), lambda i, *_: (i, 0)),
                  pl.BlockSpec((g * t, LANES), lambda i, *_: (i, 0)),
                  pl.BlockSpec((None, SUBLANES, d), lambda i, *_: (i // steps_per_batch, 0, 0)),
                  pl.BlockSpec((1, d), lambda i, *_: (0, 0)),
                  pl.BlockSpec(memory_space=pl.ANY)],
        out_specs=pl.BlockSpec((g * t, d), lambda i, *_: (i, 0)),
        scratch_shapes=[pltpu.VMEM((2 * g, DISPATCH_ROWS, d), BF16),
                        pltpu.SemaphoreType.DMA((2 * g,))],
    )
    return pl.pallas_call(
        functools.partial(_combine_kernel, g=g),
        grid_spec=grid_spec,
        out_shape=jax.ShapeDtypeStruct((n, d), F32),
        compiler_params=_cparams("arbitrary"),
        name="combine",
    )(loc.reshape(-1), gdst.reshape(-1), nch.reshape(-1), rows, x3, route, dmap, mod, fg, ys)


def _routing_tables(cnt, n_tokens):
    n_tiles = cnt.shape[0]
    i32 = lambda a: a.astype(jnp.int32)
    npad = (cnt + SEG_ALIGN - 1) // SEG_ALIGN * SEG_ALIGN
    loc = jnp.cumsum(npad, axis=1) - npad
    tot = jnp.sum(npad, axis=0)
    nblk = (tot + EXPERT_RB - 1) // EXPERT_RB
    ends = jnp.cumsum(nblk)
    base = (ends - nblk) * EXPERT_RB
    gdst = base[None, :] + jnp.cumsum(npad, axis=0) - npad
    nch = npad // SEG_ALIGN
    n_blk = (2 * n_tokens + (SEG_ALIGN - 1) * N_EXPERTS * n_tiles) // EXPERT_RB + N_EXPERTS
    blk_exp = jnp.minimum(jnp.searchsorted(ends, jnp.arange(n_blk), side="right"), N_EXPERTS - 1)
    tail = jnp.concatenate([base + tot, (nblk * EXPERT_RB - tot) // SEG_ALIGN, ends[-1:]])
    rows = jnp.sum(npad, axis=1)
    return n_blk, i32(loc), i32(gdst), i32(nch), i32(rows), i32(tail), i32(blk_exp), i32(ends[-1:])


def _rope_tables(n):
    t = jnp.arange(n)
    pos = jnp.stack([t // GRID_W, t % GRID_W], axis=1).astype(F32)
    freqs = ROPE_BASE ** (-jnp.arange(0, ROPE_AXIS_DIM, 2, dtype=F32) / ROPE_AXIS_DIM)
    ang = pos[:, :, None] * freqs
    ang = jnp.broadcast_to(ang[:, :, None, :], (n, 2, 2, ROPE_AXIS_DIM // 2))
    sign = jnp.array([-1.0, 1.0], F32)[None, None, :, None]
    cos = jnp.cos(ang).reshape(n, HEAD_DIM)
    sin = (jnp.sin(ang) * sign).reshape(n, HEAD_DIM)
    return jnp.tile(cos, (1, LANES // HEAD_DIM)), jnp.tile(sin, (1, LANES // HEAD_DIM))


def _tile_sizes(s):
    return min(512, s), min(512, s), min(2048, s)


def _pad_rows(m, rows):
    return jnp.pad(m, ((0, 0),) * (m.ndim - 2) + ((0, rows - m.shape[-2]), (0, 0)))


def kernel(x, c, ctx, c_ctx, w_mod, b_mod, norm_g, final_norm_g, w_mix_in, q_norm_g, k_norm_g, conv_w, w_mix_out, ffn_w_gate, ffn_w_up, ffn_w_down, pool_w, pool_scale, router_w, router_b, exp_w_gate, exp_w_up, exp_w_down):
    b, s, d = x.shape
    tm, tq, tk = _tile_sizes(s)

    rows = -(-(b + 1) // SUBLANES) * SUBLANES
    cvec = _pad_rows(jnp.concatenate([c, c_ctx[None, :]], axis=0), rows)
    mods = _adaln(cvec, w_mod, b_mod).reshape(w_mod.shape[0], rows, N_MOD, d)
    mod_lat = [_pad_rows(mods[l, :b], SUBLANES) for l in range(2)]
    mod_ctx0 = _pad_rows(mods[0, b], SUBLANES)

    w_in = w_mix_in[0].astype(BF16)
    cuts = (ATTN_W, ATTN_W + KV_W, ATTN_W + 2 * KV_W)
    wq, wkv, wc = w_in[:, :cuts[0]], w_in[:, cuts[0]:cuts[2]], w_in[:, cuts[2]:]
    head = jnp.arange(ATTN_W) // HEAD_DIM
    bd = ((head[:, None] == head[None, :]).astype(F32) / HEAD_DIM).astype(BF16)
    qg = jnp.tile(q_norm_g[0], N_HEADS)[None, :]
    kg = jnp.tile(k_norm_g[0], N_KV_HEADS)[None, :]
    cw = _pad_rows(conv_w[0], SUBLANES)
    cos, sin = _rope_tables(s)
    ng0a, ng0b = norm_g[0, 0][None, :], norm_g[0, 1][None, :]

    shift = (Q_SCALE * HEAD_DIM * SHIFT_MARGIN) * jnp.max(jnp.abs(q_norm_g[0])) * jnp.max(jnp.abs(k_norm_g[0]))
    qbias = jnp.where(jnp.arange(LANES - HEAD_DIM)[:, None] == 0, -shift, 0.0).astype(F32) * jnp.ones((1, tm), F32)
    q, kext, vext, conv = _l0_in(x, mod_lat[0], ng0a, wq, wkv, wc, bd, qg, kg, cw, cos, sin, qbias, tm)
    kext_c, vext_c = _ctx_kv(ctx, mod_ctx0, ng0a, wkv, bd, kg)
    attn = lax.cond(
        shift <= MAX_FIXED_SHIFT,
        lambda *a: _attention(*a, tq, tk, False),
        lambda *a: _attention(*a, tq, tk, True),
        q, kext_c, vext_c, kext, vext)

    w_out = w_mix_out[0].astype(BF16)
    n_chunks = 1
    dff = ffn_w_gate.shape[-1]
    split_cols = lambda w: w.astype(BF16).reshape(d, n_chunks, dff // n_chunks).transpose(1, 0, 2)
    wd0 = ffn_w_down[0].astype(BF16).reshape(n_chunks, dff // n_chunks, d)
    x2 = _l0_out_ffn(x, attn, conv, mod_lat[0], ng0b, w_out[:ATTN_W], w_out[ATTN_W:],
                     split_cols(ffn_w_gate[0]), split_cols(ffn_w_up[0]), wd0, tm)

    rw = jnp.pad(router_w[0], ((0, 0), (0, LANES - N_EXPERTS)))
    rb = jnp.pad(router_b[0], (0, LANES - N_EXPERTS))[None, :]
    x3, f2, route, cnt = _pool_router(x2, mod_lat[1], norm_g[1, 0][None, :], norm_g[1, 1][None, :],
                                      pool_w[0].astype(BF16), pool_scale[0][None, :], rw, rb, tm)
    n = b * s
    n_blk, loc, gdst, nch, rows, tail, blk_exp, n_valid = _routing_tables(
        cnt.reshape(n // DISPATCH_T, LANES)[:, :N_EXPERTS], n)
    route = route.reshape(n, LANES)
    tiles_per_batch = s // DISPATCH_T
    g = max(k for k in (4, 2, 1) if tiles_per_batch % k == 0)
    xs, dmap = _dispatch(f2.reshape(n, d), route, loc, gdst, nch, rows, tail, n_blk, g)
    wgu = jnp.concatenate([exp_w_gate[0].astype(BF16), exp_w_up[0].astype(BF16)], axis=-1)
    ys = _experts(xs, blk_exp, n_valid, wgu, exp_w_down[0].astype(BF16))
    out = _combine(x3.reshape(n, d), route, dmap, ys, mod_lat[1], final_norm_g[None, :],
                   loc, gdst, nch, rows, tiles_per_batch // g, g)
    return out.reshape(b, s, d)
```

```python
import functools
import math

import jax
import jax.numpy as jnp
from jax import lax
from jax.experimental import pallas as pl
from jax.experimental.pallas import tpu as pltpu

F32 = jnp.float32
BF16 = jnp.bfloat16

D_MODEL = 1024
GRID_W = 64
N_HEADS = 8
N_KV_HEADS = 2
HEAD_DIM = 64
ATTN_W = N_HEADS * HEAD_DIM
KV_W = N_KV_HEADS * HEAD_DIM
CONV_W = D_MODEL - ATTN_W
ROPE_AXIS_DIM = HEAD_DIM // 2
ROPE_BASE = 10000.0
POOL_WINDOWS = (2, 4, 8, 16)
POOL_GROUP_W = D_MODEL // len(POOL_WINDOWS)
POOL_HALO = 8
N_EXPERTS = 8
DISPATCH_T = 256
SEG_ALIGN = 16
DISPATCH_ROWS = 2 * DISPATCH_T + N_EXPERTS * SEG_ALIGN
EXPERT_RB = 512
N_MOD = 6
EPS = 1e-6

LANES = 128
SUBLANES = 8
VMEM_LIMIT = 56 * 1024 * 1024

Q_SCALE = (HEAD_DIM ** -0.5) * math.log2(math.e)
SHIFT_MARGIN = 1.01
MAX_FIXED_SHIFT = 48.0


def _cparams(*sem):
    return pltpu.CompilerParams(dimension_semantics=sem, vmem_limit_bytes=VMEM_LIMIT)


def _const_spec(shape):
    nd = len(shape)
    return pl.BlockSpec(shape, lambda *_: (0,) * nd, pipeline_mode=pl.Buffered(1))


def _rms_mod(x, g, shift, scale):
    ms = jnp.mean(x * x, axis=-1, keepdims=True)
    return x * lax.rsqrt(ms + EPS) * (g * (1.0 + scale)) + shift


def _head_rms(z, bd, g):
    ms = jnp.dot((z * z).astype(BF16), bd, preferred_element_type=F32)
    return z * lax.rsqrt(ms + EPS) * g


def _rope128(x, cos, sin_signed):
    lane = lax.broadcasted_iota(jnp.int32, x.shape, 1)
    partner = jnp.where((lane & 16) == 0, pltpu.roll(x, LANES - 16, 1), pltpu.roll(x, 16, 1))
    return x * cos + partner * sin_signed


def _adaln_kernel(c_ref, w_ref, b_ref, o_ref):
    c = c_ref[...]
    s = c * (1.0 / (1.0 + jnp.exp(-c)))
    o_ref[...] = jnp.dot(s, w_ref[...], preferred_element_type=F32,
                         precision=lax.Precision.HIGHEST) + b_ref[...]


def _adaln(cvec, w_mod, b_mod):
    depth, d, n = w_mod.shape
    rows = cvec.shape[0]
    tn = n // 4
    return pl.pallas_call(
        _adaln_kernel,
        grid=(depth, n // tn),
        in_specs=[pl.BlockSpec((rows, d), lambda l, j: (0, 0)),
                  pl.BlockSpec((None, d, tn), lambda l, j: (l, 0, j)),
                  pl.BlockSpec((None, 1, tn), lambda l, j: (l, 0, j))],
        out_specs=pl.BlockSpec((None, rows, tn), lambda l, j: (l, 0, j)),
        out_shape=jax.ShapeDtypeStruct((depth, rows, n), F32),
        compiler_params=_cparams("parallel", "parallel"),
        name="adaln",
    )(cvec, w_mod, b_mod.reshape(depth, 1, n))


def _kv_outputs(zk, zv, kext_ref, vext_ref):
    rows = zk.shape[0]
    lane = lax.broadcasted_iota(jnp.int32, zk.shape, 1)
    pad = jnp.where(lane == HEAD_DIM, 1.0, 0.0)
    zk_swapped = pltpu.roll(zk, HEAD_DIM, 1)
    vt = zv.T
    sub = lax.broadcasted_iota(jnp.int32, (LANES - HEAD_DIM, rows), 0)
    ones_row = jnp.where(sub == 0, 1.0, 0.0)
    for g in range(N_KV_HEADS):
        kg = zk if g == 0 else zk_swapped
        kext_ref[g] = jnp.where(lane < HEAD_DIM, kg, pad).astype(BF16)
        vext_ref[g] = jnp.concatenate([vt[g * HEAD_DIM:(g + 1) * HEAD_DIM, :], ones_row], axis=0).astype(BF16)


def _l0_in_kernel(x_ref, xp_ref, xn_ref, mod_ref, ng_ref, wq_ref, wkv_ref, wc_ref,
                  bd_ref, qg_ref, kg_ref, cw_ref, cos_ref, sin_ref, qb_ref,
                  q_ref, kext_ref, vext_ref, conv_ref, *, tm):
    i = pl.program_id(1)
    ni = pl.num_programs(1)
    g = ng_ref[...]
    shift, scale = mod_ref[0:1, :], mod_ref[1:2, :]
    a = _rms_mod(x_ref[...], g, shift, scale).astype(BF16)
    cos, sin = cos_ref[...], sin_ref[...]

    zq = jnp.dot(a, wq_ref[...], preferred_element_type=F32)
    qn = _head_rms(zq, bd_ref[...], qg_ref[...])
    qbias = qb_ref[...]
    for c in range(ATTN_W // LANES):
        rt = (_rope128(qn[:, c * LANES:(c + 1) * LANES], cos, sin) * Q_SCALE).T
        for hh in range(2):
            q_ref[2 * c + hh] = jnp.concatenate(
                [rt[hh * HEAD_DIM:(hh + 1) * HEAD_DIM, :], qbias], axis=0).astype(BF16)

    zkv = jnp.dot(a, wkv_ref[...], preferred_element_type=F32)
    kn = _rope128(_head_rms(zkv[:, 0:KV_W], bd_ref[0:KV_W, 0:KV_W], kg_ref[...]), cos, sin)
    _kv_outputs(kn, zkv[:, KV_W:2 * KV_W], kext_ref, vext_ref)

    zc = jnp.dot(a, wc_ref[...], preferred_element_type=F32)
    gb, w = zc[:, 0:CONV_W], zc[:, CONV_W:2 * CONV_W] * zc[:, 2 * CONV_W:3 * CONV_W]
    halo = jnp.concatenate([xp_ref[...], xn_ref[...]], axis=0)
    ah = _rms_mod(halo, g, shift, scale).astype(BF16)
    zh = jnp.dot(ah, wc_ref[:, CONV_W:3 * CONV_W], preferred_element_type=F32)
    wh = zh[:, 0:CONV_W] * zh[:, CONV_W:2 * CONV_W]
    w_before = wh[SUBLANES - 1:SUBLANES, :] * (i > 0).astype(F32)
    w_after = wh[SUBLANES:SUBLANES + 1, :] * (i < ni - 1).astype(F32)
    row = lax.broadcasted_iota(jnp.int32, w.shape, 0)
    w_prev = jnp.where(row == 0, w_before, pltpu.roll(w, 1, 0))
    w_next = jnp.where(row == tm - 1, w_after, pltpu.roll(w, tm - 1, 0))
    y = gb * (w_prev * cw_ref[0:1, :] + w * cw_ref[1:2, :] + w_next * cw_ref[2:3, :])
    conv_ref[...] = y.astype(BF16)


def _l0_in(x, mod, ng, wq, wkv, wc, bd, qg, kg, cw, cos, sin, qbias, tm):
    b, s, d = x.shape
    nt = s // tm
    hb = tm // SUBLANES
    return pl.pallas_call(
        functools.partial(_l0_in_kernel, tm=tm),
        grid=(b, nt),
        in_specs=[
            pl.BlockSpec((None, tm, d), lambda bi, i: (bi, i, 0)),
            pl.BlockSpec((None, SUBLANES, d), lambda bi, i: (bi, jnp.maximum(i * hb - 1, 0), 0)),
            pl.BlockSpec((None, SUBLANES, d),
                         lambda bi, i: (bi, jnp.minimum((i + 1) * hb, nt * hb - 1), 0)),
            pl.BlockSpec((None, SUBLANES, d), lambda bi, i: (bi, 0, 0)),
            _const_spec((1, d)),
            _const_spec(wq.shape), _const_spec(wkv.shape),
            _const_spec(wc.shape), _const_spec(bd.shape),
            _const_spec(qg.shape), _const_spec(kg.shape), _const_spec(cw.shape),
            pl.BlockSpec((tm, LANES), lambda bi, i: (i, 0)),
            pl.BlockSpec((tm, LANES), lambda bi, i: (i, 0)),
            _const_spec(qbias.shape),
        ],
        out_specs=[
            pl.BlockSpec((None, N_HEADS, LANES, tm), lambda bi, i: (bi, 0, 0, i)),
            pl.BlockSpec((None, N_KV_HEADS, tm, LANES), lambda bi, i: (bi, 0, i, 0)),
            pl.BlockSpec((None, N_KV_HEADS, LANES, tm), lambda bi, i: (bi, 0, 0, i)),
            pl.BlockSpec((None, tm, CONV_W), lambda bi, i: (bi, i, 0)),
        ],
        out_shape=[
            jax.ShapeDtypeStruct((b, N_HEADS, LANES, s), BF16),
            jax.ShapeDtypeStruct((b, N_KV_HEADS, s, LANES), BF16),
            jax.ShapeDtypeStruct((b, N_KV_HEADS, LANES, s), BF16),
            jax.ShapeDtypeStruct((b, s, CONV_W), BF16),
        ],
        compiler_params=_cparams("parallel", "parallel"),
        name="l0_in",
    )(x, x, x, mod, ng, wq, wkv, wc, bd, qg, kg, cw, cos, sin, qbias)


def _ctx_kv_kernel(x_ref, mod_ref, ng_ref, wkv_ref, bd_ref, kg_ref, kext_ref, vext_ref):
    a = _rms_mod(x_ref[...], ng_ref[...], mod_ref[0:1, :], mod_ref[1:2, :]).astype(BF16)
    zkv = jnp.dot(a, wkv_ref[...], preferred_element_type=F32)
    kn = _head_rms(zkv[:, 0:KV_W], bd_ref[0:KV_W, 0:KV_W], kg_ref[...])
    _kv_outputs(kn, zkv[:, KV_W:2 * KV_W], kext_ref, vext_ref)


def _ctx_kv(ctx, mod, ng, wkv, bd, kg):
    b, nc, d = ctx.shape
    return pl.pallas_call(
        _ctx_kv_kernel,
        grid=(b,),
        in_specs=[
            pl.BlockSpec((None, nc, d), lambda bi: (bi, 0, 0)),
            _const_spec(mod.shape), _const_spec((1, d)),
            _const_spec(wkv.shape), _const_spec(bd.shape), _const_spec(kg.shape),
        ],
        out_specs=[
            pl.BlockSpec((None, N_KV_HEADS, nc, LANES), lambda bi: (bi, 0, 0, 0)),
            pl.BlockSpec((None, N_KV_HEADS, LANES, nc), lambda bi: (bi, 0, 0, 0)),
        ],
        out_shape=[
            jax.ShapeDtypeStruct((b, N_KV_HEADS, nc, LANES), BF16),
            jax.ShapeDtypeStruct((b, N_KV_HEADS, LANES, nc), BF16),
        ],
        compiler_params=_cparams("parallel"),
        name="ctx_kv",
    )(ctx, mod, ng, wkv, bd, kg)


def _attn_kernel(q_ref, kc_ref, vc_ref, k_ref, v_ref, o_ref, qs_ref, acc_ref, *m_scratch,
                 tq, tk, nk, online):
    n_h = q_ref.shape[0]
    for h in range(n_h):
        qs_ref[:, h * tq:(h + 1) * tq] = q_ref[h]

    if online:
        m_ref, = m_scratch
        m_ref[...] = jnp.full(m_ref.shape, -jnp.inf, F32)
        acc_ref[...] = jnp.zeros(acc_ref.shape, F32)

    def step(kx, vt, first):
        s = jnp.dot(kx, qs_ref[...], preferred_element_type=F32)
        if online:
            m_prev = m_ref[...]
            m_new = jnp.maximum(m_prev, jnp.max(s, axis=0, keepdims=True))
            p = jnp.exp2(s - m_new[0:1, :]).astype(BF16)
            acc_ref[...] = (jnp.exp2(m_prev - m_new)[0:1, :] * acc_ref[...]
                            + jnp.dot(vt, p, preferred_element_type=F32))
            m_ref[...] = m_new
        else:
            pv = jnp.dot(vt, jnp.exp2(s).astype(BF16), preferred_element_type=F32)
            if first:
                acc_ref[...] = pv
            else:
                acc_ref[...] += pv

    step(kc_ref[...], vc_ref[...], True)

    def body(c, carry):
        off = pl.multiple_of(c * tk, tk)
        step(k_ref[pl.ds(off, tk), :], v_ref[:, pl.ds(off, tk)], False)
        return carry

    lax.fori_loop(0, nk, body, 0, unroll=not online)

    acc = acc_ref[...]
    r = acc[0:HEAD_DIM, :] / acc[HEAD_DIM:HEAD_DIM + 1, :]
    for p in range(n_h // 2):
        pair = jnp.concatenate([r[:, (2 * p) * tq:(2 * p + 1) * tq],
                                r[:, (2 * p + 1) * tq:(2 * p + 2) * tq]], axis=0)
        o_ref[:, p * LANES:(p + 1) * LANES] = pair.T.astype(BF16)


def _attention(q, kext_c, vext_c, kext, vext, tq, tk, online):
    b, _, _, s = q.shape
    nc = kext_c.shape[-2]
    n_h = N_HEADS // N_KV_HEADS
    cols = n_h * tq
    scratch = [pltpu.VMEM((LANES, cols), BF16), pltpu.VMEM((LANES, cols), F32)]
    if online:
        scratch.append(pltpu.VMEM((SUBLANES, cols), F32))
    return pl.pallas_call(
        functools.partial(_attn_kernel, tq=tq, tk=tk, nk=s // tk, online=online),
        grid=(b, N_KV_HEADS, s // tq),
        in_specs=[
            pl.BlockSpec((None, n_h, LANES, tq), lambda bi, g, i: (bi, g, 0, i)),
            pl.BlockSpec((None, None, nc, LANES), lambda bi, g, i: (bi, g, 0, 0)),
            pl.BlockSpec((None, None, LANES, nc), lambda bi, g, i: (bi, g, 0, 0)),
            pl.BlockSpec((None, None, s, LANES), lambda bi, g, i: (bi, g, 0, 0)),
            pl.BlockSpec((None, None, LANES, s), lambda bi, g, i: (bi, g, 0, 0)),
        ],
        out_specs=pl.BlockSpec((None, tq, n_h * HEAD_DIM), lambda bi, g, i: (bi, i, g)),
        out_shape=jax.ShapeDtypeStruct((b, s, ATTN_W), BF16),
        scratch_shapes=scratch,
        compiler_params=_cparams("parallel", "parallel", "parallel"),
        name="attention_online" if online else "attention",
    )(q, kext_c, vext_c, kext, vext)


def _l0_out_ffn_kernel(x_ref, attn_ref, conv_ref, mod_ref, ng_ref, wa_ref, wc_ref, wg_ref, wu_ref, wd_ref,
                       o_ref):
    y = (jnp.dot(attn_ref[...], wa_ref[...], preferred_element_type=F32)
         + jnp.dot(conv_ref[...], wc_ref[...], preferred_element_type=F32))
    x1 = x_ref[...] + mod_ref[2:3, :] * y
    f = _rms_mod(x1, ng_ref[...], mod_ref[3:4, :], mod_ref[4:5, :]).astype(BF16)
    hg = jnp.dot(f, wg_ref[...], preferred_element_type=F32)
    hu = jnp.dot(f, wu_ref[...], preferred_element_type=F32)
    h = (hg * (1.0 / (1.0 + jnp.exp(-hg))) * hu).astype(BF16)
    o_ref[...] = x1 + mod_ref[5:6, :] * jnp.dot(h, wd_ref[...], preferred_element_type=F32)


def _l0_out_ffn(x, attn, conv, mod, ng, wa, wc, wg, wu, wd, tm):
    b, s, d = x.shape
    tile = lambda w: pl.BlockSpec((None, tm, w), lambda bi, i: (bi, i, 0))
    return pl.pallas_call(
        _l0_out_ffn_kernel,
        grid=(b, s // tm),
        in_specs=[tile(d), tile(ATTN_W), tile(CONV_W),
                  pl.BlockSpec((None, SUBLANES, d), lambda bi, i: (bi, 0, 0)),
                  _const_spec((1, d)), _const_spec(wa.shape), _const_spec(wc.shape),
                  _const_spec(wg.shape), _const_spec(wu.shape), _const_spec(wd.shape)],
        out_specs=tile(d),
        out_shape=jax.ShapeDtypeStruct((b, s, d), F32),
        compiler_params=_cparams("parallel", "parallel"),
        name="l0_out_ffn",
    )(x, attn, conv, mod, ng, wa, wc, wg, wu, wd)


def _pool_kernel(x_ref, xp_ref, xn_ref, mod_ref, ng1_ref, ng2_ref, pw_ref, ps_ref, rw_ref, rb_ref,
                 x3_ref, f_ref, route_ref, cnt_ref, *, tm, seq):
    i = pl.program_id(1)
    ni = pl.num_programs(1)
    g = ng1_ref[...]
    shift, scale = mod_ref[0:1, :], mod_ref[1:2, :]
    x = x_ref[...]
    a = _rms_mod(x, g, shift, scale)
    a_before = _rms_mod(xp_ref[...], g, shift, scale) * (i > 0).astype(F32)
    a_after = _rms_mod(xn_ref[...], g, shift, scale) * (i < ni - 1).astype(F32)
    ext = jnp.concatenate([a_before, a, a_after], axis=0)

    t = i * tm + lax.broadcasted_iota(jnp.int32, (tm, 1), 0)
    ys = []
    for gi, w in enumerate(POOL_WINDOWS):
        sl = slice(gi * POOL_GROUP_W, (gi + 1) * POOL_GROUP_W)
        e = ext[:, sl]
        acc, span = e, 1
        while span < w:
            acc = acc + pltpu.roll(acc, span, 0)
            span *= 2
        win = acc[POOL_HALO + w // 2 - 1:POOL_HALO + w // 2 - 1 + tm, :]
        cnt = jnp.minimum(t + w // 2, seq) - jnp.maximum(t - w // 2, 0)
        p = (win / cnt.astype(F32) - a[:, sl]).astype(BF16)
        ys.append(jnp.dot(p, pw_ref[gi], preferred_element_type=F32))
    y = jnp.concatenate(ys, axis=1) * ps_ref[...]

    x3 = x + mod_ref[2:3, :] * y
    x3_ref[...] = x3
    f = _rms_mod(x3, ng2_ref[...], mod_ref[3:4, :], mod_ref[4:5, :])
    f_hi = f.astype(BF16)
    f_ref[...] = f_hi
    f_lo = (f - f_hi.astype(F32)).astype(BF16)
    z_hi = jnp.dot(f_hi, rw_ref[...], preferred_element_type=F32)
    logits = (z_hi[:, 0:LANES] + z_hi[:, LANES:2 * LANES]
              + jnp.dot(f_lo, rw_ref[:, 0:LANES], preferred_element_type=F32)) + rb_ref[...]
    lane = lax.broadcasted_iota(jnp.int32, logits.shape, 1)
    neg = jnp.float32(-jnp.inf)
    logits = jnp.where(lane < N_EXPERTS, logits, neg)
    v1 = jnp.max(logits, axis=-1, keepdims=True)
    i1 = jnp.min(jnp.where(logits == v1, lane, LANES), axis=-1, keepdims=True)
    rest = jnp.where(lane == i1, neg, logits)
    v2 = jnp.max(rest, axis=-1, keepdims=True)
    i2 = jnp.min(jnp.where(rest == v2, lane, LANES), axis=-1, keepdims=True)
    e2 = jnp.exp(v2 - v1)
    p1 = 1.0 / (1.0 + e2)
    p2 = e2 / (1.0 + e2)
    i2 = jnp.where(p2 != 0.0, i2, -1)
    route_ref[...] = jnp.where(lane == 0, i1.astype(F32), jnp.where(
        lane == 1, i2.astype(F32), jnp.where(lane == 2, p1, jnp.where(lane == 3, p2, 0.0))))
    routed = jnp.where((lane == i1) | (lane == i2), 1.0, 0.0)
    for h in range(tm // DISPATCH_T):
        cnt = jnp.sum(routed[h * DISPATCH_T:(h + 1) * DISPATCH_T, :], axis=0, keepdims=True)
        cnt_ref[h:h + 1, :] = cnt.astype(jnp.int32)


def _pool_router(x2, mod, ng1, ng2, pw, ps, rw, rb, tm):
    b, s, d = x2.shape
    nt = s // tm
    hb = tm // POOL_HALO
    rwh = rw.astype(BF16)
    rw2 = jnp.concatenate([rwh, (rw - rwh.astype(F32)).astype(BF16)], axis=1)
    tile = lambda w: pl.BlockSpec((None, tm, w), lambda bi, i: (bi, i, 0))
    return pl.pallas_call(
        functools.partial(_pool_kernel, tm=tm, seq=s),
        grid=(b, nt),
        in_specs=[
            tile(d),
            pl.BlockSpec((None, POOL_HALO, d), lambda bi, i: (bi, jnp.maximum(i * hb - 1, 0), 0)),
            pl.BlockSpec((None, POOL_HALO, d),
                         lambda bi, i: (bi, jnp.minimum((i + 1) * hb, nt * hb - 1), 0)),
            pl.BlockSpec((None, SUBLANES, d), lambda bi, i: (bi, 0, 0)),
            _const_spec((1, d)), _const_spec((1, d)), _const_spec(pw.shape), _const_spec((1, d)),
            _const_spec(rw2.shape), _const_spec(rb.shape),
        ],
        out_specs=[tile(d), tile(d), tile(LANES),
                   pl.BlockSpec((None, None, tm // DISPATCH_T, LANES), lambda bi, i: (bi, i, 0, 0))],
        out_shape=[jax.ShapeDtypeStruct((b, s, d), F32), jax.ShapeDtypeStruct((b, s, d), BF16),
                   jax.ShapeDtypeStruct((b, s, LANES), F32),
                   jax.ShapeDtypeStruct((b, nt, tm // DISPATCH_T, LANES), jnp.int32)],
        compiler_params=_cparams("parallel", "parallel"),
        name="pool_router",
    )(x2, x2, x2, mod, ng1, ng2, pw, ps, rw2, rb)


def _seg_copy(vmem_buf, slot, hbm, loc, dst, rows, sem, to_hbm):
    n = pl.multiple_of(rows, SEG_ALIGN)
    v = vmem_buf.at[slot, pl.ds(pl.multiple_of(loc, SEG_ALIGN), n)]
    h = hbm.at[pl.ds(pl.multiple_of(dst, SEG_ALIGN), n)]
    return pltpu.make_async_copy(v, h, sem) if to_hbm else pltpu.make_async_copy(h, v, sem)


def _start_segments(vmem_buf, slot, hbm, loc_s, gdst_s, nch_s, tile, sem, to_hbm):
    for e in range(N_EXPERTS):
        k = tile * N_EXPERTS + e
        rows = nch_s[k] * SEG_ALIGN

        @pl.when(rows > 0)
        def _(k=k, rows=rows):
            _seg_copy(vmem_buf, slot, hbm, loc_s[k], gdst_s[k], rows, sem, to_hbm).start()


def _wait_segments(vmem_buf, slot, hbm, rows, sem, to_hbm):
    @pl.when(rows > 0)
    def _():
        _seg_copy(vmem_buf, slot, hbm, 0, 0, rows, sem, to_hbm).wait()


def _dispatch_kernel(loc_s, gdst_s, nch_s, rows_s, tail_s, f_ref, route_ref, locrow_ref, tril_ref,
                     xs_hbm, d_ref, sbuf, zbuf, sem, zsem, *, g):
    i = pl.program_id(0)
    n_steps = pl.num_programs(0)
    t = DISPATCH_T
    base, other = (i % 2) * g, (1 - i % 2) * g
    row = lax.broadcasted_iota(jnp.int32, (DISPATCH_ROWS, t), 0).astype(F32)
    for sub in range(g):
        rows = slice(sub * t, (sub + 1) * t)
        route = route_ref[rows, :]
        lane = lax.broadcasted_iota(jnp.int32, route.shape, 1).astype(F32)
        first, second = lane == route[:, 0:1], lane == route[:, 1:2]
        onehot = jnp.where(first | second, 1.0, 0.0).astype(BF16)
        earlier = jnp.dot(tril_ref[...], onehot, preferred_element_type=F32)
        row_of = earlier + locrow_ref[sub]
        d_first = jnp.sum(jnp.where(first, row_of, 0.0), axis=1, keepdims=True)
        d_second = jnp.sum(jnp.where(second, row_of, 0.0), axis=1, keepdims=True)
        d_second = jnp.where(route[:, 1:2] < 0.0, -1.0, d_second)
        d = jnp.where(lane == 0.0, d_first, jnp.where(lane == 1.0, d_second, -1.0))
        d_ref[rows, :] = d
        dt = d.T
        perm = jnp.where((row == dt[0:1, :]) | (row == dt[1:2, :]), 1.0, 0.0)
        sbuf[base + sub] = jnp.dot(perm.astype(BF16), f_ref[rows, :], preferred_element_type=F32).astype(BF16)

    def copies(tile, slot, issue):
        if issue:
            _start_segments(sbuf, slot, xs_hbm, loc_s, gdst_s, nch_s, tile, sem.at[slot], True)
        else:
            _wait_segments(sbuf, slot, xs_hbm, rows_s[tile], sem.at[slot], True)

    for sub in range(g):
        copies(i * g + sub, base + sub, True)

    @pl.when(i > 0)
    def _():
        for sub in range(g):
            copies((i - 1) * g + sub, other + sub, False)

    @pl.when(i == n_steps - 1)
    def _():
        zbuf[...] = jnp.zeros(zbuf.shape, BF16)
        n_blk = xs_hbm.shape[0] // EXPERT_RB

        def gap_copy(e, j):
            dst = pl.multiple_of(tail_s[e] + j * SEG_ALIGN, SEG_ALIGN)
            return pltpu.make_async_copy(zbuf.at[pl.ds(0, SEG_ALIGN)], xs_hbm.at[pl.ds(dst, SEG_ALIGN)], zsem)

        def blk_copy(blk):
            dst = pl.multiple_of(blk * EXPERT_RB, EXPERT_RB)
            return pltpu.make_async_copy(zbuf, xs_hbm.at[pl.ds(dst, EXPERT_RB)], zsem)

        def fill(issue):
            def run(copy):
                copy.start() if issue else copy.wait()

            for e in range(N_EXPERTS):
                def gap_body(j, carry, e=e):
                    run(gap_copy(e, j))
                    return carry

                lax.fori_loop(0, tail_s[N_EXPERTS + e], gap_body, 0)

            def blk_body(blk, carry):
                run(blk_copy(blk))
                return carry

            lax.fori_loop(tail_s[2 * N_EXPERTS], n_blk, blk_body, 0)

        fill(True)
        for sub in range(g):
            copies(i * g + sub, base + sub, False)
        fill(False)


def _dispatch(f, route, loc, gdst, nch, rows, tail, n_blk, g):
    n, d = f.shape
    t = DISPATCH_T
    locrow = jnp.pad(loc.astype(F32), ((0, 0), (0, LANES - N_EXPERTS)))[:, None, :]
    tril = (jnp.arange(t)[:, None] > jnp.arange(t)[None, :]).astype(BF16)
    grid_spec = pltpu.PrefetchScalarGridSpec(
        num_scalar_prefetch=5,
        grid=(n // (g * t),),
        in_specs=[pl.BlockSpec((g * t, d), lambda i, *_: (i, 0)),
                  pl.BlockSpec((g * t, LANES), lambda i, *_: (i, 0)),
                  pl.BlockSpec((g, 1, LANES), lambda i, *_: (i, 0, 0)),
                  pl.BlockSpec((t, t), lambda i, *_: (0, 0))],
        out_specs=[pl.BlockSpec(memory_space=pl.ANY),
                   pl.BlockSpec((g * t, LANES), lambda i, *_: (i, 0))],
        scratch_shapes=[pltpu.VMEM((2 * g, DISPATCH_ROWS, d), BF16),
                        pltpu.VMEM((EXPERT_RB, d), BF16),
                        pltpu.SemaphoreType.DMA((2 * g,)),
                        pltpu.SemaphoreType.DMA(())],
    )
    return pl.pallas_call(
        functools.partial(_dispatch_kernel, g=g),
        grid_spec=grid_spec,
        out_shape=[jax.ShapeDtypeStruct((n_blk * EXPERT_RB, d), BF16),
                   jax.ShapeDtypeStruct((n, LANES), F32)],
        compiler_params=_cparams("arbitrary"),
        name="dispatch",
    )(loc.reshape(-1), gdst.reshape(-1), nch.reshape(-1), rows, tail, f, route, locrow, tril)


def _expert_kernel(exp_s, nvalid_s, xs_ref, wgu_ref, wd_ref, ys_ref):
    used = pl.program_id(0) < nvalid_s[0]

    @pl.when(used)
    def _():
        dff = wd_ref.shape[0]
        z = jnp.dot(xs_ref[...], wgu_ref[...], preferred_element_type=F32)
        hg, hu = z[:, 0:dff], z[:, dff:2 * dff]
        h = (hg * (1.0 / (1.0 + jnp.exp(-hg))) * hu).astype(BF16)
        ys_ref[...] = jnp.dot(h, wd_ref[...], preferred_element_type=F32).astype(BF16)

    @pl.when(jnp.logical_not(used))
    def _():
        ys_ref[...] = jnp.zeros(ys_ref.shape, BF16)


def _experts(xs, blk_exp, n_valid, wgu, wd):
    rows, d = xs.shape
    dff = wd.shape[1]
    grid_spec = pltpu.PrefetchScalarGridSpec(
        num_scalar_prefetch=2,
        grid=(rows // EXPERT_RB,),
        in_specs=[pl.BlockSpec((EXPERT_RB, d), lambda b, exp_s, nv: (b, 0)),
                  pl.BlockSpec((None, d, 2 * dff), lambda b, exp_s, nv: (exp_s[b], 0, 0)),
                  pl.BlockSpec((None, dff, d), lambda b, exp_s, nv: (exp_s[b], 0, 0))],
        out_specs=pl.BlockSpec((EXPERT_RB, d), lambda b, exp_s, nv: (b, 0)),
    )
    return pl.pallas_call(
        _expert_kernel,
        grid_spec=grid_spec,
        out_shape=jax.ShapeDtypeStruct((rows, d), BF16),
        compiler_params=_cparams("arbitrary"),
        name="experts",
    )(blk_exp, n_valid, xs, wgu, wd)


def _combine_kernel(loc_s, gdst_s, nch_s, rows_s, x_ref, route_ref, d_ref, mod_ref, fg_ref, ys_hbm,
                    o_ref, ybuf, sem, *, g):
    i = pl.program_id(0)
    n_steps = pl.num_programs(0)
    t = DISPATCH_T
    base, other = (i % 2) * g, (1 - i % 2) * g

    def copies(tile, slot, issue):
        if issue:
            _start_segments(ybuf, slot, ys_hbm, loc_s, gdst_s, nch_s, tile, sem.at[slot], False)
        else:
            _wait_segments(ybuf, slot, ys_hbm, rows_s[tile], sem.at[slot], False)

    @pl.when(i == 0)
    def _():
        ybuf[...] = jnp.zeros(ybuf.shape, BF16)
        for sub in range(g):
            copies(sub, base + sub, True)

    @pl.when(i + 1 < n_steps)
    def _():
        for sub in range(g):
            copies((i + 1) * g + sub, other + sub, True)

    for sub in range(g):
        copies(i * g + sub, base + sub, False)
    lane = lax.broadcasted_iota(jnp.int32, (t, DISPATCH_ROWS), 1).astype(F32)
    for sub in range(g):
        rows = slice(sub * t, (sub + 1) * t)
        route, d = route_ref[rows, :], d_ref[rows, :]
        gmat = jnp.where(lane == d[:, 0:1], route[:, 2:3], jnp.where(lane == d[:, 1:2], route[:, 3:4], 0.0))
        moe = jnp.dot(gmat.astype(BF16), ybuf[base + sub], preferred_element_type=F32)
        x4 = x_ref[rows, :] + mod_ref[5:6, :] * moe
        ms = jnp.mean(x4 * x4, axis=-1, keepdims=True)
        o_ref[rows, :] = x4 * lax.rsqrt(ms + EPS) * fg_ref[...]


def _combine(x3, route, dmap, ys, mod, fg, loc, gdst, nch, rows, steps_per_batch, g):
    n, d = x3.shape
    t = DISPATCH_T
    grid_spec = pltpu.PrefetchScalarGridSpec(
        num_scalar_prefetch=4,
        grid=(n // (g * t),),
        in_specs=[pl.BlockSpec((g * t, d), lambda i, *_: (i, 0)),
                  pl.BlockSpec((g * t, LANES), lambda i, *_: (i, 0)),
                  pl.BlockSpec((g * t, LANES), lambda i, *_: (i, 0)),
                  pl.BlockSpec((None, SUBLANES, d), lambda i, *_: (i // steps_per_batch, 0, 0)),
                  pl.BlockSpec((1, d), lambda i, *_: (0, 0)),
                  pl.BlockSpec(memory_space=pl.ANY)],
        out_specs=pl.BlockSpec((g * t, d), lambda i, *_: (i, 0)),
        scratch_shapes=[pltpu.VMEM((2 * g, DISPATCH_ROWS, d), BF16),
                        pltpu.SemaphoreType.DMA((2 * g,))],
    )
    return pl.pallas_call(
        functools.partial(_combine_kernel, g=g),
        grid_spec=grid_spec,
        out_shape=jax.ShapeDtypeStruct((n, d), F32),
        compiler_params=_cparams("arbitrary"),
        name="combine",
    )(loc.reshape(-1), gdst.reshape(-1), nch.reshape(-1), rows, x3, route, dmap, mod, fg, ys)


def _routing_tables(cnt, n_tokens):
    n_tiles = cnt.shape[0]
    i32 = lambda a: a.astype(jnp.int32)
    npad = (cnt + SEG_ALIGN - 1) // SEG_ALIGN * SEG_ALIGN
    loc = jnp.cumsum(npad, axis=1) - npad
    tot = jnp.sum(npad, axis=0)
    nblk = (tot + EXPERT_RB - 1) // EXPERT_RB
    ends = jnp.cumsum(nblk)
    base = (ends - nblk) * EXPERT_RB
    gdst = base[None, :] + jnp.cumsum(npad, axis=0) - npad
    nch = npad // SEG_ALIGN
    n_blk = (2 * n_tokens + (SEG_ALIGN - 1) * N_EXPERTS * n_tiles) // EXPERT_RB + N_EXPERTS
    blk_exp = jnp.minimum(jnp.searchsorted(ends, jnp.arange(n_blk), side="right"), N_EXPERTS - 1)
    tail = jnp.concatenate([base + tot, (nblk * EXPERT_RB - tot) // SEG_ALIGN, ends[-1:]])
    rows = jnp.sum(npad, axis=1)
    return n_blk, i32(loc), i32(gdst), i32(nch), i32(rows), i32(tail), i32(blk_exp), i32(ends[-1:])


def _rope_tables(n):
    t = jnp.arange(n)
    pos = jnp.stack([t // GRID_W, t % GRID_W], axis=1).astype(F32)
    freqs = ROPE_BASE ** (-jnp.arange(0, ROPE_AXIS_DIM, 2, dtype=F32) / ROPE_AXIS_DIM)
    ang = pos[:, :, None] * freqs
    ang = jnp.broadcast_to(ang[:, :, None, :], (n, 2, 2, ROPE_AXIS_DIM // 2))
    sign = jnp.array([-1.0, 1.0], F32)[None, None, :, None]
    cos = jnp.cos(ang).reshape(n, HEAD_DIM)
    sin = (jnp.sin(ang) * sign).reshape(n, HEAD_DIM)
    return jnp.tile(cos, (1, LANES // HEAD_DIM)), jnp.tile(sin, (1, LANES // HEAD_DIM))


def _tile_sizes(s):
    return min(512, s), min(512, s), min(2048, s)


def _pad_rows(m, rows):
    return jnp.pad(m, ((0, 0),) * (m.ndim - 2) + ((0, rows - m.shape[-2]), (0, 0)))


def kernel(x, c, ctx, c_ctx, w_mod, b_mod, norm_g, final_norm_g, w_mix_in, q_norm_g, k_norm_g, conv_w, w_mix_out, ffn_w_gate, ffn_w_up, ffn_w_down, pool_w, pool_scale, router_w, router_b, exp_w_gate, exp_w_up, exp_w_down):
    b, s, d = x.shape
    tm, tq, tk = _tile_sizes(s)

    rows = -(-(b + 1) // SUBLANES) * SUBLANES
    cvec = _pad_rows(jnp.concatenate([c, c_ctx[None, :]], axis=0), rows)
    mods = _adaln(cvec, w_mod, b_mod).reshape(w_mod.shape[0], rows, N_MOD, d)
    mod_lat = [_pad_rows(mods[l, :b], SUBLANES) for l in range(2)]
    mod_ctx0 = _pad_rows(mods[0, b], SUBLANES)

    w_in = w_mix_in[0].astype(BF16)
    cuts = (ATTN_W, ATTN_W + KV_W, ATTN_W + 2 * KV_W)
    wq, wkv, wc = w_in[:, :cuts[0]], w_in[:, cuts[0]:cuts[2]], w_in[:, cuts[2]:]
    head = jnp.arange(ATTN_W) // HEAD_DIM
    bd = ((head[:, None] == head[None, :]).astype(F32) / HEAD_DIM).astype(BF16)
    qg = jnp.tile(q_norm_g[0], N_HEADS)[None, :]
    kg = jnp.tile(k_norm_g[0], N_KV_HEADS)[None, :]
    cw = _pad_rows(conv_w[0], SUBLANES)
    cos, sin = _rope_tables(s)
    ng0a, ng0b = norm_g[0, 0][None, :], norm_g[0, 1][None, :]

    shift = (Q_SCALE * HEAD_DIM * SHIFT_MARGIN) * jnp.max(jnp.abs(q_norm_g[0])) * jnp.max(jnp.abs(k_norm_g[0]))
    qbias = jnp.where(jnp.arange(LANES - HEAD_DIM)[:, None] == 0, -shift, 0.0).astype(F32) * jnp.ones((1, tm), F32)
    q, kext, vext, conv = _l0_in(x, mod_lat[0], ng0a, wq, wkv, wc, bd, qg, kg, cw, cos, sin, qbias, tm)
    kext_c, vext_c = _ctx_kv(ctx, mod_ctx0, ng0a, wkv, bd, kg)
    attn = lax.cond(
        shift <= MAX_FIXED_SHIFT,
        lambda *a: _attention(*a, tq, tk, False),
        lambda *a: _attention(*a, tq, tk, True),
        q, kext_c, vext_c, kext, vext)

    w_out = w_mix_out[0].astype(BF16)
    x2 = _l0_out_ffn(x, attn, conv, mod_lat[0], ng0b, w_out[:ATTN_W], w_out[ATTN_W:],
                     ffn_w_gate[0].astype(BF16), ffn_w_up[0].astype(BF16), ffn_w_down[0].astype(BF16), tm)

    rw = jnp.pad(router_w[0], ((0, 0), (0, LANES - N_EXPERTS)))
    rb = jnp.pad(router_b[0], (0, LANES - N_EXPERTS))[None, :]
    x3, f2, route, cnt = _pool_router(x2, mod_lat[1], norm_g[1, 0][None, :], norm_g[1, 1][None, :],
                                      pool_w[0].astype(BF16), pool_scale[0][None, :], rw, rb, tm)
    n = b * s
    n_blk, loc, gdst, nch, rows, tail, blk_exp, n_valid = _routing_tables(
        cnt.reshape(n // DISPATCH_T, LANES)[:, :N_EXPERTS], n)
    route = route.reshape(n, LANES)
    tiles_per_batch = s // DISPATCH_T
    g = max(k for k in (4, 2, 1) if tiles_per_batch % k == 0)
    xs, dmap = _dispatch(f2.reshape(n, d), route, loc, gdst, nch, rows, tail, n_blk, g)
    wgu = jnp.concatenate([exp_w_gate[0], exp_w_up[0]], axis=-1).astype(BF16)
    ys = _experts(xs, blk_exp, n_valid, wgu, exp_w_down[0].astype(BF16))
    out = _combine(x3.reshape(n, d), route, dmap, ys, mod_lat[1], final_norm_g[None, :],
                   loc, gdst, nch, rows, tiles_per_batch // g, g)
    return out.reshape(b, s, d)
```

```python
import functools
import math

import jax
import jax.numpy as jnp
from jax import lax
from jax.experimental import pallas as pl
from jax.experimental.pallas import tpu as pltpu

F32 = jnp.float32
BF16 = jnp.bfloat16

D_MODEL = 1024
GRID_W = 64
N_HEADS = 8
N_KV_HEADS = 2
HEAD_DIM = 64
ATTN_W = N_HEADS * HEAD_DIM
KV_W = N_KV_HEADS * HEAD_DIM
CONV_W = D_MODEL - ATTN_W
ROPE_AXIS_DIM = HEAD_DIM // 2
ROPE_BASE = 10000.0
POOL_WINDOWS = (2, 4, 8, 16)
POOL_GROUP_W = D_MODEL // len(POOL_WINDOWS)
POOL_HALO = 8
N_EXPERTS = 8
DISPATCH_T = 256
SEG_ALIGN = 16
DISPATCH_ROWS = 2 * DISPATCH_T + N_EXPERTS * SEG_ALIGN
EXPERT_RB = 512
N_MOD = 6
EPS = 1e-6

LANES = 128
SUBLANES = 8
VMEM_LIMIT = 56 * 1024 * 1024

Q_SCALE = (HEAD_DIM ** -0.5) * math.log2(math.e)
SHIFT_MARGIN = 1.01
MAX_FIXED_SHIFT = 48.0


def _cparams(*sem):
    return pltpu.CompilerParams(dimension_semantics=sem, vmem_limit_bytes=VMEM_LIMIT)


def _const_spec(shape):
    nd = len(shape)
    return pl.BlockSpec(shape, lambda *_: (0,) * nd, pipeline_mode=pl.Buffered(1))


def _rms_mod(x, g, shift, scale):
    ms = jnp.mean(x * x, axis=-1, keepdims=True)
    return x * lax.rsqrt(ms + EPS) * (g * (1.0 + scale)) + shift


def _head_rms(z, bd, g):
    ms = jnp.dot((z * z).astype(BF16), bd, preferred_element_type=F32)
    return z * lax.rsqrt(ms + EPS) * g


def _rope128(x, cos, sin_signed):
    lane = lax.broadcasted_iota(jnp.int32, x.shape, 1)
    partner = jnp.where((lane & 16) == 0, pltpu.roll(x, LANES - 16, 1), pltpu.roll(x, 16, 1))
    return x * cos + partner * sin_signed


def _adaln_kernel(c_ref, w_ref, b_ref, o_ref):
    c = c_ref[...]
    s = c * (1.0 / (1.0 + jnp.exp(-c)))
    o_ref[...] = jnp.dot(s, w_ref[...], preferred_element_type=F32,
                         precision=lax.Precision.HIGHEST) + b_ref[...]


def _adaln(cvec, w_mod, b_mod):
    depth, d, n = w_mod.shape
    rows = cvec.shape[0]
    tn = n // 4
    return pl.pallas_call(
        _adaln_kernel,
        grid=(depth, n // tn),
        in_specs=[pl.BlockSpec((rows, d), lambda l, j: (0, 0)),
                  pl.BlockSpec((None, d, tn), lambda l, j: (l, 0, j)),
                  pl.BlockSpec((None, 1, tn), lambda l, j: (l, 0, j))],
        out_specs=pl.BlockSpec((None, rows, tn), lambda l, j: (l, 0, j)),
        out_shape=jax.ShapeDtypeStruct((depth, rows, n), F32),
        compiler_params=_cparams("parallel", "parallel"),
        name="adaln",
    )(cvec, w_mod, b_mod.reshape(depth, 1, n))


def _kv_outputs(zk, zv, kext_ref, vext_ref):
    rows = zk.shape[0]
    lane = lax.broadcasted_iota(jnp.int32, zk.shape, 1)
    pad = jnp.where(lane == HEAD_DIM, 1.0, 0.0)
    zk_swapped = pltpu.roll(zk, HEAD_DIM, 1)
    vt = zv.T
    sub = lax.broadcasted_iota(jnp.int32, (LANES - HEAD_DIM, rows), 0)
    ones_row = jnp.where(sub == 0, 1.0, 0.0)
    for g in range(N_KV_HEADS):
        kg = zk if g == 0 else zk_swapped
        kext_ref[g] = jnp.where(lane < HEAD_DIM, kg, pad).astype(BF16)
        vext_ref[g] = jnp.concatenate([vt[g * HEAD_DIM:(g + 1) * HEAD_DIM, :], ones_row], axis=0).astype(BF16)


def _l0_in_kernel(x_ref, xp_ref, xn_ref, mod_ref, ng_ref, wq_ref, wkv_ref, wc_ref,
                  bd_ref, qg_ref, kg_ref, cw_ref, cos_ref, sin_ref, qb_ref,
                  q_ref, kext_ref, vext_ref, conv_ref, *, tm):
    i = pl.program_id(1)
    ni = pl.num_programs(1)
    g = ng_ref[...]
    shift, scale = mod_ref[0:1, :], mod_ref[1:2, :]
    a = _rms_mod(x_ref[...], g, shift, scale).astype(BF16)
    cos, sin = cos_ref[...], sin_ref[...]

    zq = jnp.dot(a, wq_ref[...], preferred_element_type=F32)
    qn = _head_rms(zq, bd_ref[...], qg_ref[...])
    qbias = qb_ref[...]
    for c in range(ATTN_W // LANES):
        rt = (_rope128(qn[:, c * LANES:(c + 1) * LANES], cos, sin) * Q_SCALE).T
        for hh in range(2):
            q_ref[2 * c + hh] = jnp.concatenate(
                [rt[hh * HEAD_DIM:(hh + 1) * HEAD_DIM, :], qbias], axis=0).astype(BF16)

    zkv = jnp.dot(a, wkv_ref[...], preferred_element_type=F32)
    kn = _rope128(_head_rms(zkv[:, 0:KV_W], bd_ref[0:KV_W, 0:KV_W], kg_ref[...]), cos, sin)
    _kv_outputs(kn, zkv[:, KV_W:2 * KV_W], kext_ref, vext_ref)

    zc = jnp.dot(a, wc_ref[...], preferred_element_type=F32)
    gb, w = zc[:, 0:CONV_W], zc[:, CONV_W:2 * CONV_W] * zc[:, 2 * CONV_W:3 * CONV_W]
    halo = jnp.concatenate([xp_ref[...], xn_ref[...]], axis=0)
    ah = _rms_mod(halo, g, shift, scale).astype(BF16)
    zh = jnp.dot(ah, wc_ref[:, CONV_W:3 * CONV_W], preferred_element_type=F32)
    wh = zh[:, 0:CONV_W] * zh[:, CONV_W:2 * CONV_W]
    w_before = wh[SUBLANES - 1:SUBLANES, :] * (i > 0).astype(F32)
    w_after = wh[SUBLANES:SUBLANES + 1, :] * (i < ni - 1).astype(F32)
    row = lax.broadcasted_iota(jnp.int32, w.shape, 0)
    w_prev = jnp.where(row == 0, w_before, pltpu.roll(w, 1, 0))
    w_next = jnp.where(row == tm - 1, w_after, pltpu.roll(w, tm - 1, 0))
    y = gb * (w_prev * cw_ref[0:1, :] + w * cw_ref[1:2, :] + w_next * cw_ref[2:3, :])
    conv_ref[...] = y.astype(BF16)


def _l0_in(x, mod, ng, wq, wkv, wc, bd, qg, kg, cw, cos, sin, qbias, tm):
    b, s, d = x.shape
    nt = s // tm
    hb = tm // SUBLANES
    return pl.pallas_call(
        functools.partial(_l0_in_kernel, tm=tm),
        grid=(b, nt),
        in_specs=[
            pl.BlockSpec((None, tm, d), lambda bi, i: (bi, i, 0)),
            pl.BlockSpec((None, SUBLANES, d), lambda bi, i: (bi, jnp.maximum(i * hb - 1, 0), 0)),
            pl.BlockSpec((None, SUBLANES, d),
                         lambda bi, i: (bi, jnp.minimum((i + 1) * hb, nt * hb - 1), 0)),
            pl.BlockSpec((None, SUBLANES, d), lambda bi, i: (bi, 0, 0)),
            _const_spec((1, d)),
            _const_spec(wq.shape), _const_spec(wkv.shape),
            _const_spec(wc.shape), _const_spec(bd.shape),
            _const_spec(qg.shape), _const_spec(kg.shape), _const_spec(cw.shape),
            pl.BlockSpec((tm, LANES), lambda bi, i: (i, 0)),
            pl.BlockSpec((tm, LANES), lambda bi, i: (i, 0)),
            _const_spec(qbias.shape),
        ],
        out_specs=[
            pl.BlockSpec((None, N_HEADS, LANES, tm), lambda bi, i: (bi, 0, 0, i)),
            pl.BlockSpec((None, N_KV_HEADS, tm, LANES), lambda bi, i: (bi, 0, i, 0)),
            pl.BlockSpec((None, N_KV_HEADS, LANES, tm), lambda bi, i: (bi, 0, 0, i)),
            pl.BlockSpec((None, tm, CONV_W), lambda bi, i: (bi, i, 0)),
        ],
        out_shape=[
            jax.ShapeDtypeStruct((b, N_HEADS, LANES, s), BF16),
            jax.ShapeDtypeStruct((b, N_KV_HEADS, s, LANES), BF16),
            jax.ShapeDtypeStruct((b, N_KV_HEADS, LANES, s), BF16),
            jax.ShapeDtypeStruct((b, s, CONV_W), BF16),
        ],
        compiler_params=_cparams("parallel", "parallel"),
        name="l0_in",
    )(x, x, x, mod, ng, wq, wkv, wc, bd, qg, kg, cw, cos, sin, qbias)


def _ctx_kv_kernel(x_ref, mod_ref, ng_ref, wkv_ref, bd_ref, kg_ref, kext_ref, vext_ref):
    a = _rms_mod(x_ref[...], ng_ref[...], mod_ref[0:1, :], mod_ref[1:2, :]).astype(BF16)
    zkv = jnp.dot(a, wkv_ref[...], preferred_element_type=F32)
    kn = _head_rms(zkv[:, 0:KV_W], bd_ref[0:KV_W, 0:KV_W], kg_ref[...])
    _kv_outputs(kn, zkv[:, KV_W:2 * KV_W], kext_ref, vext_ref)


def _ctx_kv(ctx, mod, ng, wkv, bd, kg):
    b, nc, d = ctx.shape
    return pl.pallas_call(
        _ctx_kv_kernel,
        grid=(b,),
        in_specs=[
            pl.BlockSpec((None, nc, d), lambda bi: (bi, 0, 0)),
            _const_spec(mod.shape), _const_spec((1, d)),
            _const_spec(wkv.shape), _const_spec(bd.shape), _const_spec(kg.shape),
        ],
        out_specs=[
            pl.BlockSpec((None, N_KV_HEADS, nc, LANES), lambda bi: (bi, 0, 0, 0)),
            pl.BlockSpec((None, N_KV_HEADS, LANES, nc), lambda bi: (bi, 0, 0, 0)),
        ],
        out_shape=[
            jax.ShapeDtypeStruct((b, N_KV_HEADS, nc, LANES), BF16),
            jax.ShapeDtypeStruct((b, N_KV_HEADS, LANES, nc), BF16),
        ],
        compiler_params=_cparams("parallel"),
        name="ctx_kv",
    )(ctx, mod, ng, wkv, bd, kg)


def _attn_kernel(q_ref, kc_ref, vc_ref, k_ref, v_ref, o_ref, qs_ref, acc_ref, *m_scratch,
                 tq, tk, nk, online):
    n_h = q_ref.shape[0]
    for h in range(n_h):
        qs_ref[:, h * tq:(h + 1) * tq] = q_ref[h]

    if online:
        m_ref, = m_scratch
        m_ref[...] = jnp.full(m_ref.shape, -jnp.inf, F32)
        acc_ref[...] = jnp.zeros(acc_ref.shape, F32)

    def step(kx, vt, first):
        s = jnp.dot(kx, qs_ref[...], preferred_element_type=F32)
        if online:
            m_prev = m_ref[...]
            m_new = jnp.maximum(m_prev, jnp.max(s, axis=0, keepdims=True))
            p = jnp.exp2(s - m_new[0:1, :]).astype(BF16)
            acc_ref[...] = (jnp.exp2(m_prev - m_new)[0:1, :] * acc_ref[...]
                            + jnp.dot(vt, p, preferred_element_type=F32))
            m_ref[...] = m_new
        else:
            pv = jnp.dot(vt, jnp.exp2(s).astype(BF16), preferred_element_type=F32)
            if first:
                acc_ref[...] = pv
            else:
                acc_ref[...] += pv

    step(kc_ref[...], vc_ref[...], True)

    def body(c, carry):
        off = pl.multiple_of(c * tk, tk)
        step(k_ref[pl.ds(off, tk), :], v_ref[:, pl.ds(off, tk)], False)
        return carry

    lax.fori_loop(0, nk, body, 0, unroll=not online)

    acc = acc_ref[...]
    r = acc[0:HEAD_DIM, :] / acc[HEAD_DIM:HEAD_DIM + 1, :]
    for p in range(n_h // 2):
        pair = jnp.concatenate([r[:, (2 * p) * tq:(2 * p + 1) * tq],
                                r[:, (2 * p + 1) * tq:(2 * p + 2) * tq]], axis=0)
        o_ref[:, p * LANES:(p + 1) * LANES] = pair.T.astype(BF16)


def _attention(q, kext_c, vext_c, kext, vext, tq, tk, online):
    b, _, _, s = q.shape
    nc = kext_c.shape[-2]
    n_h = N_HEADS // N_KV_HEADS
    cols = n_h * tq
    scratch = [pltpu.VMEM((LANES, cols), BF16), pltpu.VMEM((LANES, cols), F32)]
    if online:
        scratch.append(pltpu.VMEM((SUBLANES, cols), F32))
    return pl.pallas_call(
        functools.partial(_attn_kernel, tq=tq, tk=tk, nk=s // tk, online=online),
        grid=(b, N_KV_HEADS, s // tq),
        in_specs=[
            pl.BlockSpec((None, n_h, LANES, tq), lambda bi, g, i: (bi, g, 0, i)),
            pl.BlockSpec((None, None, nc, LANES), lambda bi, g, i: (bi, g, 0, 0)),
            pl.BlockSpec((None, None, LANES, nc), lambda bi, g, i: (bi, g, 0, 0)),
            pl.BlockSpec((None, None, s, LANES), lambda bi, g, i: (bi, g, 0, 0)),
            pl.BlockSpec((None, None, LANES, s), lambda bi, g, i: (bi, g, 0, 0)),
        ],
        out_specs=pl.BlockSpec((None, tq, n_h * HEAD_DIM), lambda bi, g, i: (bi, i, g)),
        out_shape=jax.ShapeDtypeStruct((b, s, ATTN_W), BF16),
        scratch_shapes=scratch,
        compiler_params=_cparams("parallel", "parallel", "parallel"),
        name="attention_online" if online else "attention",
    )(q, kext_c, vext_c, kext, vext)


def _l0_out_ffn_kernel(x_ref, attn_ref, conv_ref, mod_ref, ng_ref, wa_ref, wc_ref, wg_ref, wu_ref, wd_ref,
                       o_ref):
    y = (jnp.dot(attn_ref[...], wa_ref[...], preferred_element_type=F32)
         + jnp.dot(conv_ref[...], wc_ref[...], preferred_element_type=F32))
    x1 = x_ref[...] + mod_ref[2:3, :] * y
    f = _rms_mod(x1, ng_ref[...], mod_ref[3:4, :], mod_ref[4:5, :]).astype(BF16)
    hg = jnp.dot(f, wg_ref[...], preferred_element_type=F32)
    hu = jnp.dot(f, wu_ref[...], preferred_element_type=F32)
    h = (hg * (1.0 / (1.0 + jnp.exp(-hg))) * hu).astype(BF16)
    o_ref[...] = x1 + mod_ref[5:6, :] * jnp.dot(h, wd_ref[...], preferred_element_type=F32)


def _l0_out_ffn(x, attn, conv, mod, ng, wa, wc, wg, wu, wd, tm):
    b, s, d = x.shape
    tile = lambda w: pl.BlockSpec((None, tm, w), lambda bi, i: (bi, i, 0))
    return pl.pallas_call(
        _l0_out_ffn_kernel,
        grid=(b, s // tm),
        in_specs=[tile(d), tile(ATTN_W), tile(CONV_W),
                  pl.BlockSpec((None, SUBLANES, d), lambda bi, i: (bi, 0, 0)),
                  _const_spec((1, d)), _const_spec(wa.shape), _const_spec(wc.shape),
                  _const_spec(wg.shape), _const_spec(wu.shape), _const_spec(wd.shape)],
        out_specs=tile(d),
        out_shape=jax.ShapeDtypeStruct((b, s, d), F32),
        compiler_params=_cparams("parallel", "parallel"),
        name="l0_out_ffn",
    )(x, attn, conv, mod, ng, wa, wc, wg, wu, wd)


def _pool_kernel(x_ref, xp_ref, xn_ref, mod_ref, ng1_ref, ng2_ref, pw_ref, ps_ref, rw_ref, rb_ref,
                 x3_ref, f_ref, route_ref, cnt_ref, *, tm, seq):
    i = pl.program_id(1)
    ni = pl.num_programs(1)
    g = ng1_ref[...]
    shift, scale = mod_ref[0:1, :], mod_ref[1:2, :]
    x = x_ref[...]
    a = _rms_mod(x, g, shift, scale)
    a_before = _rms_mod(xp_ref[...], g, shift, scale) * (i > 0).astype(F32)
    a_after = _rms_mod(xn_ref[...], g, shift, scale) * (i < ni - 1).astype(F32)
    ext = jnp.concatenate([a_before, a, a_after], axis=0)

    t = i * tm + lax.broadcasted_iota(jnp.int32, (tm, 1), 0)
    ys = []
    for gi, w in enumerate(POOL_WINDOWS):
        sl = slice(gi * POOL_GROUP_W, (gi + 1) * POOL_GROUP_W)
        e = ext[:, sl]
        acc, span = e, 1
        while span < w:
            acc = acc + pltpu.roll(acc, span, 0)
            span *= 2
        win = acc[POOL_HALO + w // 2 - 1:POOL_HALO + w // 2 - 1 + tm, :]
        cnt = jnp.minimum(t + w // 2, seq) - jnp.maximum(t - w // 2, 0)
        p = (win / cnt.astype(F32) - a[:, sl]).astype(BF16)
        ys.append(jnp.dot(p, pw_ref[gi], preferred_element_type=F32))
    y = jnp.concatenate(ys, axis=1) * ps_ref[...]

    x3 = x + mod_ref[2:3, :] * y
    x3_ref[...] = x3
    f = _rms_mod(x3, ng2_ref[...], mod_ref[3:4, :], mod_ref[4:5, :])
    f_hi = f.astype(BF16)
    f_ref[...] = f_hi
    f_lo = (f - f_hi.astype(F32)).astype(BF16)
    z_hi = jnp.dot(f_hi, rw_ref[...], preferred_element_type=F32)
    logits = (z_hi[:, 0:LANES] + z_hi[:, LANES:2 * LANES]
              + jnp.dot(f_lo, rw_ref[:, 0:LANES], preferred_element_type=F32)) + rb_ref[...]
    lane = lax.broadcasted_iota(jnp.int32, logits.shape, 1)
    neg = jnp.float32(-jnp.inf)
    logits = jnp.where(lane < N_EXPERTS, logits, neg)
    v1 = jnp.max(logits, axis=-1, keepdims=True)
    i1 = jnp.min(jnp.where(logits == v1, lane, LANES), axis=-1, keepdims=True)
    rest = jnp.where(lane == i1, neg, logits)
    v2 = jnp.max(rest, axis=-1, keepdims=True)
    i2 = jnp.min(jnp.where(rest == v2, lane, LANES), axis=-1, keepdims=True)
    e2 = jnp.exp(v2 - v1)
    p1 = 1.0 / (1.0 + e2)
    p2 = e2 / (1.0 + e2)
    i2 = jnp.where(p2 != 0.0, i2, -1)
    route_ref[...] = jnp.where(lane == 0, i1.astype(F32), jnp.where(
        lane == 1, i2.astype(F32), jnp.where(lane == 2, p1, jnp.where(lane == 3, p2, 0.0))))
    routed = jnp.where((lane == i1) | (lane == i2), 1.0, 0.0)
    for h in range(tm // DISPATCH_T):
        cnt = jnp.sum(routed[h * DISPATCH_T:(h + 1) * DISPATCH_T, :], axis=0, keepdims=True)
        cnt_ref[h:h + 1, :] = cnt.astype(jnp.int32)


def _pool_router(x2, mod, ng1, ng2, pw, ps, rw, rb, tm):
    b, s, d = x2.shape
    nt = s // tm
    hb = tm // POOL_HALO
    rwh = rw.astype(BF16)
    rw2 = jnp.concatenate([rwh, (rw - rwh.astype(F32)).astype(BF16)], axis=1)
    tile = lambda w: pl.BlockSpec((None, tm, w), lambda bi, i: (bi, i, 0))
    return pl.pallas_call(
        functools.partial(_pool_kernel, tm=tm, seq=s),
        grid=(b, nt),
        in_specs=[
            tile(d),
            pl.BlockSpec((None, POOL_HALO, d), lambda bi, i: (bi, jnp.maximum(i * hb - 1, 0), 0)),
            pl.BlockSpec((None, POOL_HALO, d),
                         lambda bi, i: (bi, jnp.minimum((i + 1) * hb, nt * hb - 1), 0)),
            pl.BlockSpec((None, SUBLANES, d), lambda bi, i: (bi, 0, 0)),
            _const_spec((1, d)), _const_spec((1, d)), _const_spec(pw.shape), _const_spec((1, d)),
            _const_spec(rw2.shape), _const_spec(rb.shape),
        ],
        out_specs=[tile(d), tile(d), tile(LANES),
                   pl.BlockSpec((None, None, tm // DISPATCH_T, LANES), lambda bi, i: (bi, i, 0, 0))],
        out_shape=[jax.ShapeDtypeStruct((b, s, d), F32), jax.ShapeDtypeStruct((b, s, d), BF16),
                   jax.ShapeDtypeStruct((b, s, LANES), F32),
                   jax.ShapeDtypeStruct((b, nt, tm // DISPATCH_T, LANES), jnp.int32)],
        compiler_params=_cparams("parallel", "parallel"),
        name="pool_router",
    )(x2, x2, x2, mod, ng1, ng2, pw, ps, rw2, rb)


def _seg_copy(vmem_buf, slot, hbm, loc, dst, rows, sem, to_hbm):
    n = pl.multiple_of(rows, SEG_ALIGN)
    v = vmem_buf.at[slot, pl.ds(pl.multiple_of(loc, SEG_ALIGN), n)]
    h = hbm.at[pl.ds(pl.multiple_of(dst, SEG_ALIGN), n)]
    return pltpu.make_async_copy(v, h, sem) if to_hbm else pltpu.make_async_copy(h, v, sem)


def _start_segments(vmem_buf, slot, hbm, loc_s, gdst_s, nch_s, tile, sem, to_hbm):
    for e in range(N_EXPERTS):
        k = tile * N_EXPERTS + e
        rows = nch_s[k] * SEG_ALIGN

        @pl.when(rows > 0)
        def _(k=k, rows=rows):
            _seg_copy(vmem_buf, slot, hbm, loc_s[k], gdst_s[k], rows, sem, to_hbm).start()


def _wait_segments(vmem_buf, slot, hbm, rows, sem, to_hbm):
    @pl.when(rows > 0)
    def _():
        _seg_copy(vmem_buf, slot, hbm, 0, 0, rows, sem, to_hbm).wait()


def _dispatch_kernel(loc_s, gdst_s, nch_s, rows_s, tail_s, f_ref, route_ref, locrow_ref, tril_ref,
                     xs_hbm, d_ref, sbuf, zbuf, sem, zsem, *, g):
    i = pl.program_id(0)
    n_steps = pl.num_programs(0)
    t = DISPATCH_T
    base, other = (i % 2) * g, (1 - i % 2) * g
    row = lax.broadcasted_iota(jnp.int32, (DISPATCH_ROWS, t), 0).astype(F32)
    for sub in range(g):
        rows = slice(sub * t, (sub + 1) * t)
        route = route_ref[rows, :]
        lane = lax.broadcasted_iota(jnp.int32, route.shape, 1).astype(F32)
        first, second = lane == route[:, 0:1], lane == route[:, 1:2]
        onehot = jnp.where(first | second, 1.0, 0.0).astype(BF16)
        earlier = jnp.dot(tril_ref[...], onehot, preferred_element_type=F32)
        row_of = earlier + locrow_ref[sub]
        d_first = jnp.sum(jnp.where(first, row_of, 0.0), axis=1, keepdims=True)
        d_second = jnp.sum(jnp.where(second, row_of, 0.0), axis=1, keepdims=True)
        d_second = jnp.where(route[:, 1:2] < 0.0, -1.0, d_second)
        d = jnp.where(lane == 0.0, d_first, jnp.where(lane == 1.0, d_second, -1.0))
        d_ref[rows, :] = d
        dt = d.T
        perm = jnp.where((row == dt[0:1, :]) | (row == dt[1:2, :]), 1.0, 0.0)
        sbuf[base + sub] = jnp.dot(perm.astype(BF16), f_ref[rows, :], preferred_element_type=F32).astype(BF16)

    def copies(tile, slot, issue):
        if issue:
            _start_segments(sbuf, slot, xs_hbm, loc_s, gdst_s, nch_s, tile, sem.at[slot], True)
        else:
            _wait_segments(sbuf, slot, xs_hbm, rows_s[tile], sem.at[slot], True)

    for sub in range(g):
        copies(i * g + sub, base + sub, True)

    @pl.when(i > 0)
    def _():
        for sub in range(g):
            copies((i - 1) * g + sub, other + sub, False)

    @pl.when(i == n_steps - 1)
    def _():
        zbuf[...] = jnp.zeros(zbuf.shape, BF16)
        n_blk = xs_hbm.shape[0] // EXPERT_RB

        def gap_copy(e, j):
            dst = pl.multiple_of(tail_s[e] + j * SEG_ALIGN, SEG_ALIGN)
            return pltpu.make_async_copy(zbuf.at[pl.ds(0, SEG_ALIGN)], xs_hbm.at[pl.ds(dst, SEG_ALIGN)], zsem)

        def blk_copy(blk):
            dst = pl.multiple_of(blk * EXPERT_RB, EXPERT_RB)
            return pltpu.make_async_copy(zbuf, xs_hbm.at[pl.ds(dst, EXPERT_RB)], zsem)

        def fill(issue):
            def run(copy):
                copy.start() if issue else copy.wait()

            for e in range(N_EXPERTS):
                def gap_body(j, carry, e=e):
                    run(gap_copy(e, j))
                    return carry

                lax.fori_loop(0, tail_s[N_EXPERTS + e], gap_body, 0)

            def blk_body(blk, carry):
                run(blk_copy(blk))
                return carry

            lax.fori_loop(tail_s[2 * N_EXPERTS], n_blk, blk_body, 0)

        fill(True)
        for sub in range(g):
            copies(i * g + sub, base + sub, False)
        fill(False)


def _dispatch(f, route, loc, gdst, nch, rows, tail, n_blk, g):
    n, d = f.shape
    t = DISPATCH_T
    locrow = jnp.pad(loc.astype(F32), ((0, 0), (0, LANES - N_EXPERTS)))[:, None, :]
    tril = (jnp.arange(t)[:, None] > jnp.arange(t)[None, :]).astype(BF16)
    grid_spec = pltpu.PrefetchScalarGridSpec(
        num_scalar_prefetch=5,
        grid=(n // (g * t),),
        in_specs=[pl.BlockSpec((g * t, d), lambda i, *_: (i, 0)),
                  pl.BlockSpec((g * t, LANES), lambda i, *_: (i, 0)),
                  pl.BlockSpec((g, 1, LANES), lambda i, *_: (i, 0, 0)),
                  pl.BlockSpec((t, t), lambda i, *_: (0, 0))],
        out_specs=[pl.BlockSpec(memory_space=pl.ANY),
                   pl.BlockSpec((g * t, LANES), lambda i, *_: (i, 0))],
        scratch_shapes=[pltpu.VMEM((2 * g, DISPATCH_ROWS, d), BF16),
                        pltpu.VMEM((EXPERT_RB, d), BF16),
                        pltpu.SemaphoreType.DMA((2 * g,)),
                        pltpu.SemaphoreType.DMA(())],
    )
    return pl.pallas_call(
        functools.partial(_dispatch_kernel, g=g),
        grid_spec=grid_spec,
        out_shape=[jax.ShapeDtypeStruct((n_blk * EXPERT_RB, d), BF16),
                   jax.ShapeDtypeStruct((n, LANES), F32)],
        compiler_params=_cparams("arbitrary"),
        name="dispatch",
    )(loc.reshape(-1), gdst.reshape(-1), nch.reshape(-1), rows, tail, f, route, locrow, tril)


def _expert_kernel(exp_s, nvalid_s, xs_ref, wgu_ref, wd_ref, ys_ref):
    used = pl.program_id(0) < nvalid_s[0]

    @pl.when(used)
    def _():
        dff = wd_ref.shape[0]
        z = jnp.dot(xs_ref[...], wgu_ref[...], preferred_element_type=F32)
        hg, hu = z[:, 0:dff], z[:, dff:2 * dff]
        h = (hg * (1.0 / (1.0 + jnp.exp(-hg))) * hu).astype(BF16)
        ys_ref[...] = jnp.dot(h, wd_ref[...], preferred_element_type=F32).astype(BF16)

    @pl.when(jnp.logical_not(used))
    def _():
        ys_ref[...] = jnp.zeros(ys_ref.shape, BF16)


def _experts(xs, blk_exp, n_valid, wgu, wd):
    rows, d = xs.shape
    dff = wd.shape[1]
    grid_spec = pltpu.PrefetchScalarGridSpec(
        num_scalar_prefetch=2,
        grid=(rows // EXPERT_RB,),
        in_specs=[pl.BlockSpec((EXPERT_RB, d), lambda b, exp_s, nv: (b, 0)),
                  pl.BlockSpec((None, d, 2 * dff), lambda b, exp_s, nv: (exp_s[b], 0, 0)),
                  pl.BlockSpec((None, dff, d), lambda b, exp_s, nv: (exp_s[b], 0, 0))],
        out_specs=pl.BlockSpec((EXPERT_RB, d), lambda b, exp_s, nv: (b, 0)),
    )
    return pl.pallas_call(
        _expert_kernel,
        grid_spec=grid_spec,
        out_shape=jax.ShapeDtypeStruct((rows, d), BF16),
        compiler_params=_cparams("arbitrary"),
        name="experts",
    )(blk_exp, n_valid, xs, wgu, wd)


def _combine_kernel(loc_s, gdst_s, nch_s, rows_s, x_ref, route_ref, d_ref, mod_ref, fg_ref, ys_hbm,
                    o_ref, ybuf, sem, *, g):
    i = pl.program_id(0)
    n_steps = pl.num_programs(0)
    t = DISPATCH_T
    base, other = (i % 2) * g, (1 - i % 2) * g

    def copies(tile, slot, issue):
        if issue:
            _start_segments(ybuf, slot, ys_hbm, loc_s, gdst_s, nch_s, tile, sem.at[slot], False)
        else:
            _wait_segments(ybuf, slot, ys_hbm, rows_s[tile], sem.at[slot], False)

    @pl.when(i == 0)
    def _():
        ybuf[...] = jnp.zeros(ybuf.shape, BF16)
        for sub in range(g):
            copies(sub, base + sub, True)

    @pl.when(i + 1 < n_steps)
    def _():
        for sub in range(g):
            copies((i + 1) * g + sub, other + sub, True)

    for sub in range(g):
        copies(i * g + sub, base + sub, False)
    lane = lax.broadcasted_iota(jnp.int32, (t, DISPATCH_ROWS), 1).astype(F32)
    for sub in range(g):
        rows = slice(sub * t, (sub + 1) * t)
        route, d = route_ref[rows, :], d_ref[rows, :]
        gmat = jnp.where(lane == d[:, 0:1], route[:, 2:3], jnp.where(lane == d[:, 1:2], route[:, 3:4], 0.0))
        moe = jnp.dot(gmat.astype(BF16), ybuf[base + sub], preferred_element_type=F32)
        x4 = x_ref[rows, :] + mod_ref[5:6, :] * moe
        ms = jnp.mean(x4 * x4, axis=-1, keepdims=True)
        o_ref[rows, :] = x4 * lax.rsqrt(ms + EPS) * fg_ref[...]


def _combine(x3, route, dmap, ys, mod, fg, loc, gdst, nch, rows, steps_per_batch, g):
    n, d = x3.shape
    t = DISPATCH_T
    grid_spec = pltpu.PrefetchScalarGridSpec(
        num_scalar_prefetch=4,
        grid=(n // (g * t),),
        in_specs=[pl.BlockSpec((g * t, d), lambda i, *_: (i, 0)),
                  pl.BlockSpec((g * t, LANES), lambda i, *_: (i, 0)),
                  pl.BlockSpec((g * t, LANES), lambda i, *_: (i, 0)),
                  pl.BlockSpec((None, SUBLANES, d), lambda i, *_: (i // steps_per_batch, 0, 0)),
                  pl.BlockSpec((1, d), lambda i, *_: (0, 0)),
                  pl.BlockSpec(memory_space=pl.ANY)],
        out_specs=pl.BlockSpec((g * t, d), lambda i, *_: (i, 0)),
        scratch_shapes=[pltpu.VMEM((2 * g, DISPATCH_ROWS, d), BF16),
                        pltpu.SemaphoreType.DMA((2 * g,))],
    )
    return pl.pallas_call(
        functools.partial(_combine_kernel, g=g),
        grid_spec=grid_spec,
        out_shape=jax.ShapeDtypeStruct((n, d), F32),
        compiler_params=_cparams("arbitrary"),
        name="combine",
    )(loc.reshape(-1), gdst.reshape(-1), nch.reshape(-1), rows, x3, route, dmap, mod, fg, ys)


def _routing_tables(cnt, n_tokens):
    n_tiles = cnt.shape[0]
    i32 = lambda a: a.astype(jnp.int32)
    npad = (cnt + SEG_ALIGN - 1) // SEG_ALIGN * SEG_ALIGN
    loc = jnp.cumsum(npad, axis=1) - npad
    tot = jnp.sum(npad, axis=0)
    nblk = (tot + EXPERT_RB - 1) // EXPERT_RB
    ends = jnp.cumsum(nblk)
    base = (ends - nblk) * EXPERT_RB
    gdst = base[None, :] + jnp.cumsum(npad, axis=0) - npad
    nch = npad // SEG_ALIGN
    n_blk = (2 * n_tokens + (SEG_ALIGN - 1) * N_EXPERTS * n_tiles) // EXPERT_RB + N_EXPERTS
    blk_exp = jnp.minimum(jnp.searchsorted(ends, jnp.arange(n_blk), side="right"), N_EXPERTS - 1)
    tail = jnp.concatenate([base + tot, (nblk * EXPERT_RB - tot) // SEG_ALIGN, ends[-1:]])
    rows = jnp.sum(npad, axis=1)
    return n_blk, i32(loc), i32(gdst), i32(nch), i32(rows), i32(tail), i32(blk_exp), i32(ends[-1:])


def _rope_tables(n):
    t = jnp.arange(n)
    pos = jnp.stack([t // GRID_W, t % GRID_W], axis=1).astype(F32)
    freqs = ROPE_BASE ** (-jnp.arange(0, ROPE_AXIS_DIM, 2, dtype=F32) / ROPE_AXIS_DIM)
    ang = pos[:, :, None] * freqs
    ang = jnp.broadcast_to(ang[:, :, None, :], (n, 2, 2, ROPE_AXIS_DIM // 2))
    sign = jnp.array([-1.0, 1.0], F32)[None, None, :, None]
    cos = jnp.cos(ang).reshape(n, HEAD_DIM)
    sin = (jnp.sin(ang) * sign).reshape(n, HEAD_DIM)
    return jnp.tile(cos, (1, LANES // HEAD_DIM)), jnp.tile(sin, (1, LANES // HEAD_DIM))


def _tile_sizes(s):
    return min(1024, s), min(512, s), min(512, s), min(2048, s)


def _pad_rows(m, rows):
    return jnp.pad(m, ((0, 0),) * (m.ndim - 2) + ((0, rows - m.shape[-2]), (0, 0)))


def kernel(x, c, ctx, c_ctx, w_mod, b_mod, norm_g, final_norm_g, w_mix_in, q_norm_g, k_norm_g, conv_w, w_mix_out, ffn_w_gate, ffn_w_up, ffn_w_down, pool_w, pool_scale, router_w, router_b, exp_w_gate, exp_w_up, exp_w_down):
    b, s, d = x.shape
    tm, tm_ffn, tq, tk = _tile_sizes(s)

    rows = -(-(b + 1) // SUBLANES) * SUBLANES
    cvec = _pad_rows(jnp.concatenate([c, c_ctx[None, :]], axis=0), rows)
    mods = _adaln(cvec, w_mod, b_mod).reshape(w_mod.shape[0], rows, N_MOD, d)
    mod_lat = [_pad_rows(mods[l, :b], SUBLANES) for l in range(2)]
    mod_ctx0 = _pad_rows(mods[0, b], SUBLANES)

    w_in = w_mix_in[0].astype(BF16)
    cuts = (ATTN_W, ATTN_W + KV_W, ATTN_W + 2 * KV_W)
    wq, wkv, wc = w_in[:, :cuts[0]], w_in[:, cuts[0]:cuts[2]], w_in[:, cuts[2]:]
    head = jnp.arange(ATTN_W) // HEAD_DIM
    bd = ((head[:, None] == head[None, :]).astype(F32) / HEAD_DIM).astype(BF16)
    qg = jnp.tile(q_norm_g[0], N_HEADS)[None, :]
    kg = jnp.tile(k_norm_g[0], N_KV_HEADS)[None, :]
    cw = _pad_rows(conv_w[0], SUBLANES)
    cos, sin = _rope_tables(s)
    ng0a, ng0b = norm_g[0, 0][None, :], norm_g[0, 1][None, :]

    shift = (Q_SCALE * HEAD_DIM * SHIFT_MARGIN) * jnp.max(jnp.abs(q_norm_g[0])) * jnp.max(jnp.abs(k_norm_g[0]))
    qbias = jnp.where(jnp.arange(LANES - HEAD_DIM)[:, None] == 0, -shift, 0.0).astype(F32) * jnp.ones((1, tm), F32)
    q, kext, vext, conv = _l0_in(x, mod_lat[0], ng0a, wq, wkv, wc, bd, qg, kg, cw, cos, sin, qbias, tm)
    kext_c, vext_c = _ctx_kv(ctx, mod_ctx0, ng0a, wkv, bd, kg)
    attn = lax.cond(
        shift <= MAX_FIXED_SHIFT,
        lambda *a: _attention(*a, tq, tk, False),
        lambda *a: _attention(*a, tq, tk, True),
        q, kext_c, vext_c, kext, vext)

    w_out = w_mix_out[0].astype(BF16)
    x2 = _l0_out_ffn(x, attn, conv, mod_lat[0], ng0b, w_out[:ATTN_W], w_out[ATTN_W:],
                     ffn_w_gate[0].astype(BF16), ffn_w_up[0].astype(BF16), ffn_w_down[0].astype(BF16), tm_ffn)

    rw = jnp.pad(router_w[0], ((0, 0), (0, LANES - N_EXPERTS)))
    rb = jnp.pad(router_b[0], (0, LANES - N_EXPERTS))[None, :]
    x3, f2, route, cnt = _pool_router(x2, mod_lat[1], norm_g[1, 0][None, :], norm_g[1, 1][None, :],
                                      pool_w[0].astype(BF16), pool_scale[0][None, :], rw, rb, tm)
    n = b * s
    n_blk, loc, gdst, nch, rows, tail, blk_exp, n_valid = _routing_tables(
        cnt.reshape(n // DISPATCH_T, LANES)[:, :N_EXPERTS], n)
    route = route.reshape(n, LANES)
    tiles_per_batch = s // DISPATCH_T
    g = max(k for k in (4, 2, 1) if tiles_per_batch % k == 0)
    xs, dmap = _dispatch(f2.reshape(n, d), route, loc, gdst, nch, rows, tail, n_blk, g)
    wgu = jnp.concatenate([exp_w_gate[0], exp_w_up[0]], axis=-1).astype(BF16)
    ys = _experts(xs, blk_exp, n_valid, wgu, exp_w_down[0].astype(BF16))
    out = _combine(x3.reshape(n, d), route, dmap, ys, mod_lat[1], final_norm_g[None, :],
                   loc, gdst, nch, rows, tiles_per_batch // g, g)
    return out.reshape(b, s, d)
```

```python
import functools
import math

import jax
import jax.numpy as jnp
from jax import lax
from jax.experimental import pallas as pl
from jax.experimental.pallas import tpu as pltpu

F32 = jnp.float32
BF16 = jnp.bfloat16

D_MODEL = 1024
GRID_W = 64
N_HEADS = 8
N_KV_HEADS = 2
HEAD_DIM = 64
ATTN_W = N_HEADS * HEAD_DIM
KV_W = N_KV_HEADS * HEAD_DIM
CONV_W = D_MODEL - ATTN_W
ROPE_AXIS_DIM = HEAD_DIM // 2
ROPE_BASE = 10000.0
POOL_WINDOWS = (2, 4, 8, 16)
POOL_GROUP_W = D_MODEL // len(POOL_WINDOWS)
POOL_HALO = 8
N_EXPERTS = 8
DISPATCH_T = 256
SEG_ALIGN = 16
DISPATCH_ROWS = 2 * DISPATCH_T + N_EXPERTS * SEG_ALIGN
EXPERT_RB = 512
N_MOD = 6
EPS = 1e-6

LANES = 128
SUBLANES = 8
VMEM_LIMIT = 56 * 1024 * 1024

Q_SCALE = (HEAD_DIM ** -0.5) * math.log2(math.e)
SHIFT_MARGIN = 1.01
MAX_FIXED_SHIFT = 48.0


def _cparams(*sem):
    return pltpu.CompilerParams(dimension_semantics=sem, vmem_limit_bytes=VMEM_LIMIT)


def _const_spec(shape):
    nd = len(shape)
    return pl.BlockSpec(shape, lambda *_: (0,) * nd, pipeline_mode=pl.Buffered(1))


def _rms_mod(x, g, shift, scale):
    ms = jnp.mean(x * x, axis=-1, keepdims=True)
    return x * lax.rsqrt(ms + EPS) * (g * (1.0 + scale)) + shift


def _head_rms(z, bd, g):
    ms = jnp.dot((z * z).astype(BF16), bd, preferred_element_type=F32)
    return z * lax.rsqrt(ms + EPS) * g


def _rope128(x, cos, sin_signed):
    lane = lax.broadcasted_iota(jnp.int32, x.shape, 1)
    partner = jnp.where((lane & 16) == 0, pltpu.roll(x, LANES - 16, 1), pltpu.roll(x, 16, 1))
    return x * cos + partner * sin_signed


def _adaln_kernel(c_ref, w_ref, b_ref, o_ref):
    c = c_ref[...]
    s = c * (1.0 / (1.0 + jnp.exp(-c)))
    o_ref[...] = jnp.dot(s, w_ref[...], preferred_element_type=F32,
                         precision=lax.Precision.HIGHEST) + b_ref[...]


def _adaln(cvec, w_mod, b_mod):
    depth, d, n = w_mod.shape
    rows = cvec.shape[0]
    tn = n // 4
    return pl.pallas_call(
        _adaln_kernel,
        grid=(depth, n // tn),
        in_specs=[pl.BlockSpec((rows, d), lambda l, j: (0, 0)),
                  pl.BlockSpec((None, d, tn), lambda l, j: (l, 0, j)),
                  pl.BlockSpec((None, 1, tn), lambda l, j: (l, 0, j))],
        out_specs=pl.BlockSpec((None, rows, tn), lambda l, j: (l, 0, j)),
        out_shape=jax.ShapeDtypeStruct((depth, rows, n), F32),
        compiler_params=_cparams("parallel", "parallel"),
        name="adaln",
    )(cvec, w_mod, b_mod.reshape(depth, 1, n))


def _kv_outputs(zk, zv, kext_ref, vext_ref):
    rows = zk.shape[0]
    lane = lax.broadcasted_iota(jnp.int32, zk.shape, 1)
    pad = jnp.where(lane == HEAD_DIM, 1.0, 0.0)
    zk_swapped = pltpu.roll(zk, HEAD_DIM, 1)
    vt = zv.T
    sub = lax.broadcasted_iota(jnp.int32, (LANES - HEAD_DIM, rows), 0)
    ones_row = jnp.where(sub == 0, 1.0, 0.0)
    for g in range(N_KV_HEADS):
        kg = zk if g == 0 else zk_swapped
        kext_ref[g] = jnp.where(lane < HEAD_DIM, kg, pad).astype(BF16)
        vext_ref[g] = jnp.concatenate([vt[g * HEAD_DIM:(g + 1) * HEAD_DIM, :], ones_row], axis=0).astype(BF16)


def _l0_in_kernel(x_ref, xp_ref, xn_ref, mod_ref, ng_ref, wq_ref, wkv_ref, wc_ref,
                  bd_ref, qg_ref, kg_ref, cw_ref, cos_ref, sin_ref, qb_ref,
                  q_ref, kext_ref, vext_ref, conv_ref, *, tm):
    i = pl.program_id(1)
    ni = pl.num_programs(1)
    g = ng_ref[...]
    shift, scale = mod_ref[0:1, :], mod_ref[1:2, :]
    a = _rms_mod(x_ref[...], g, shift, scale).astype(BF16)
    cos, sin = cos_ref[...], sin_ref[...]

    zq = jnp.dot(a, wq_ref[...], preferred_element_type=F32)
    qn = _head_rms(zq, bd_ref[...], qg_ref[...])
    qbias = qb_ref[...]
    for c in range(ATTN_W // LANES):
        rt = (_rope128(qn[:, c * LANES:(c + 1) * LANES], cos, sin) * Q_SCALE).T
        for hh in range(2):
            q_ref[2 * c + hh] = jnp.concatenate(
                [rt[hh * HEAD_DIM:(hh + 1) * HEAD_DIM, :], qbias], axis=0).astype(BF16)

    zkv = jnp.dot(a, wkv_ref[...], preferred_element_type=F32)
    kn = _rope128(_head_rms(zkv[:, 0:KV_W], bd_ref[0:KV_W, 0:KV_W], kg_ref[...]), cos, sin)
    _kv_outputs(kn, zkv[:, KV_W:2 * KV_W], kext_ref, vext_ref)

    zc = jnp.dot(a, wc_ref[...], preferred_element_type=F32)
    gb, w = zc[:, 0:CONV_W], zc[:, CONV_W:2 * CONV_W] * zc[:, 2 * CONV_W:3 * CONV_W]
    halo = jnp.concatenate([xp_ref[...], xn_ref[...]], axis=0)
    ah = _rms_mod(halo, g, shift, scale).astype(BF16)
    zh = jnp.dot(ah, wc_ref[:, CONV_W:3 * CONV_W], preferred_element_type=F32)
    wh = zh[:, 0:CONV_W] * zh[:, CONV_W:2 * CONV_W]
    w_before = wh[SUBLANES - 1:SUBLANES, :] * (i > 0).astype(F32)
    w_after = wh[SUBLANES:SUBLANES + 1, :] * (i < ni - 1).astype(F32)
    row = lax.broadcasted_iota(jnp.int32, w.shape, 0)
    w_prev = jnp.where(row == 0, w_before, pltpu.roll(w, 1, 0))
    w_next = jnp.where(row == tm - 1, w_after, pltpu.roll(w, tm - 1, 0))
    y = gb * (w_prev * cw_ref[0:1, :] + w * cw_ref[1:2, :] + w_next * cw_ref[2:3, :])
    conv_ref[...] = y.astype(BF16)


def _l0_in(x, mod, ng, wq, wkv, wc, bd, qg, kg, cw, cos, sin, qbias, tm):
    b, s, d = x.shape
    nt = s // tm
    hb = tm // SUBLANES
    return pl.pallas_call(
        functools.partial(_l0_in_kernel, tm=tm),
        grid=(b, nt),
        in_specs=[
            pl.BlockSpec((None, tm, d), lambda bi, i: (bi, i, 0)),
            pl.BlockSpec((None, SUBLANES, d), lambda bi, i: (bi, jnp.maximum(i * hb - 1, 0), 0)),
            pl.BlockSpec((None, SUBLANES, d),
                         lambda bi, i: (bi, jnp.minimum((i + 1) * hb, nt * hb - 1), 0)),
            pl.BlockSpec((None, SUBLANES, d), lambda bi, i: (bi, 0, 0)),
            _const_spec((1, d)),
            _const_spec(wq.shape), _const_spec(wkv.shape),
            _const_spec(wc.shape), _const_spec(bd.shape),
            _const_spec(qg.shape), _const_spec(kg.shape), _const_spec(cw.shape),
            pl.BlockSpec((tm, LANES), lambda bi, i: (i, 0)),
            pl.BlockSpec((tm, LANES), lambda bi, i: (i, 0)),
            _const_spec(qbias.shape),
        ],
        out_specs=[
            pl.BlockSpec((None, N_HEADS, LANES, tm), lambda bi, i: (bi, 0, 0, i)),
            pl.BlockSpec((None, N_KV_HEADS, tm, LANES), lambda bi, i: (bi, 0, i, 0)),
            pl.BlockSpec((None, N_KV_HEADS, LANES, tm), lambda bi, i: (bi, 0, 0, i)),
            pl.BlockSpec((None, tm, CONV_W), lambda bi, i: (bi, i, 0)),
        ],
        out_shape=[
            jax.ShapeDtypeStruct((b, N_HEADS, LANES, s), BF16),
            jax.ShapeDtypeStruct((b, N_KV_HEADS, s, LANES), BF16),
            jax.ShapeDtypeStruct((b, N_KV_HEADS, LANES, s), BF16),
            jax.ShapeDtypeStruct((b, s, CONV_W), BF16),
        ],
        compiler_params=_cparams("parallel", "parallel"),
        name="l0_in",
    )(x, x, x, mod, ng, wq, wkv, wc, bd, qg, kg, cw, cos, sin, qbias)


def _ctx_kv_kernel(x_ref, mod_ref, ng_ref, wkv_ref, bd_ref, kg_ref, kext_ref, vext_ref):
    a = _rms_mod(x_ref[...], ng_ref[...], mod_ref[0:1, :], mod_ref[1:2, :]).astype(BF16)
    zkv = jnp.dot(a, wkv_ref[...], preferred_element_type=F32)
    kn = _head_rms(zkv[:, 0:KV_W], bd_ref[0:KV_W, 0:KV_W], kg_ref[...])
    _kv_outputs(kn, zkv[:, KV_W:2 * KV_W], kext_ref, vext_ref)


def _ctx_kv(ctx, mod, ng, wkv, bd, kg):
    b, nc, d = ctx.shape
    return pl.pallas_call(
        _ctx_kv_kernel,
        grid=(b,),
        in_specs=[
            pl.BlockSpec((None, nc, d), lambda bi: (bi, 0, 0)),
            _const_spec(mod.shape), _const_spec((1, d)),
            _const_spec(wkv.shape), _const_spec(bd.shape), _const_spec(kg.shape),
        ],
        out_specs=[
            pl.BlockSpec((None, N_KV_HEADS, nc, LANES), lambda bi: (bi, 0, 0, 0)),
            pl.BlockSpec((None, N_KV_HEADS, LANES, nc), lambda bi: (bi, 0, 0, 0)),
        ],
        out_shape=[
            jax.ShapeDtypeStruct((b, N_KV_HEADS, nc, LANES), BF16),
            jax.ShapeDtypeStruct((b, N_KV_HEADS, LANES, nc), BF16),
        ],
        compiler_params=_cparams("parallel"),
        name="ctx_kv",
    )(ctx, mod, ng, wkv, bd, kg)


def _attn_kernel(q_ref, kc_ref, vc_ref, k_ref, v_ref, o_ref, qs_ref, acc_ref, *m_scratch,
                 tq, tk, nk, online):
    n_h = q_ref.shape[0]
    for h in range(n_h):
        qs_ref[:, h * tq:(h + 1) * tq] = q_ref[h]

    if online:
        m_ref, = m_scratch
        m_ref[...] = jnp.full(m_ref.shape, -jnp.inf, F32)
        acc_ref[...] = jnp.zeros(acc_ref.shape, F32)

    def step(kx, vt, first):
        s = jnp.dot(kx, qs_ref[...], preferred_element_type=F32)
        if online:
            m_prev = m_ref[...]
            m_new = jnp.maximum(m_prev, jnp.max(s, axis=0, keepdims=True))
            p = jnp.exp2(s - m_new[0:1, :]).astype(BF16)
            acc_ref[...] = (jnp.exp2(m_prev - m_new)[0:1, :] * acc_ref[...]
                            + jnp.dot(vt, p, preferred_element_type=F32))
            m_ref[...] = m_new
        else:
            pv = jnp.dot(vt, jnp.exp2(s).astype(BF16), preferred_element_type=F32)
            if first:
                acc_ref[...] = pv
            else:
                acc_ref[...] += pv

    step(kc_ref[...], vc_ref[...], True)

    def body(c, carry):
        off = pl.multiple_of(c * tk, tk)
        step(k_ref[pl.ds(off, tk), :], v_ref[:, pl.ds(off, tk)], False)
        return carry

    lax.fori_loop(0, nk, body, 0, unroll=not online)

    acc = acc_ref[...]
    r = acc[0:HEAD_DIM, :] / acc[HEAD_DIM:HEAD_DIM + 1, :]
    for p in range(n_h // 2):
        pair = jnp.concatenate([r[:, (2 * p) * tq:(2 * p + 1) * tq],
                                r[:, (2 * p + 1) * tq:(2 * p + 2) * tq]], axis=0)
        o_ref[:, p * LANES:(p + 1) * LANES] = pair.T.astype(BF16)


def _attention(q, kext_c, vext_c, kext, vext, tq, tk, online):
    b, _, _, s = q.shape
    nc = kext_c.shape[-2]
    n_h = N_HEADS // N_KV_HEADS
    cols = n_h * tq
    scratch = [pltpu.VMEM((LANES, cols), BF16), pltpu.VMEM((LANES, cols), F32)]
    if online:
        scratch.append(pltpu.VMEM((SUBLANES, cols), F32))
    return pl.pallas_call(
        functools.partial(_attn_kernel, tq=tq, tk=tk, nk=s // tk, online=online),
        grid=(b, N_KV_HEADS, s // tq),
        in_specs=[
            pl.BlockSpec((None, n_h, LANES, tq), lambda bi, g, i: (bi, g, 0, i)),
            pl.BlockSpec((None, None, nc, LANES), lambda bi, g, i: (bi, g, 0, 0)),
            pl.BlockSpec((None, None, LANES, nc), lambda bi, g, i: (bi, g, 0, 0)),
            pl.BlockSpec((None, None, s, LANES), lambda bi, g, i: (bi, g, 0, 0)),
            pl.BlockSpec((None, None, LANES, s), lambda bi, g, i: (bi, g, 0, 0)),
        ],
        out_specs=pl.BlockSpec((None, tq, n_h * HEAD_DIM), lambda bi, g, i: (bi, i, g)),
        out_shape=jax.ShapeDtypeStruct((b, s, ATTN_W), BF16),
        scratch_shapes=scratch,
        compiler_params=_cparams("parallel", "parallel", "parallel"),
        name="attention_online" if online else "attention",
    )(q, kext_c, vext_c, kext, vext)


def _l0_out_ffn_kernel(x_ref, attn_ref, conv_ref, mod_ref, ng_ref, wa_ref, wc_ref, wg_ref, wu_ref, wd_ref,
                       o_ref):
    y = (jnp.dot(attn_ref[...], wa_ref[...], preferred_element_type=F32)
         + jnp.dot(conv_ref[...], wc_ref[...], preferred_element_type=F32))
    x1 = x_ref[...] + mod_ref[2:3, :] * y
    f = _rms_mod(x1, ng_ref[...], mod_ref[3:4, :], mod_ref[4:5, :]).astype(BF16)
    hg = jnp.dot(f, wg_ref[...], preferred_element_type=F32)
    hu = jnp.dot(f, wu_ref[...], preferred_element_type=F32)
    h = (hg * (1.0 / (1.0 + jnp.exp(-hg))) * hu).astype(BF16)
    o_ref[...] = x1 + mod_ref[5:6, :] * jnp.dot(h, wd_ref[...], preferred_element_type=F32)


def _l0_out_ffn(x, attn, conv, mod, ng, wa, wc, wg, wu, wd, tm):
    b, s, d = x.shape
    tile = lambda w: pl.BlockSpec((None, tm, w), lambda bi, i: (bi, i, 0))
    return pl.pallas_call(
        _l0_out_ffn_kernel,
        grid=(b, s // tm),
        in_specs=[tile(d), tile(ATTN_W), tile(CONV_W),
                  pl.BlockSpec((None, SUBLANES, d), lambda bi, i: (bi, 0, 0)),
                  _const_spec((1, d)), _const_spec(wa.shape), _const_spec(wc.shape),
                  _const_spec(wg.shape), _const_spec(wu.shape), _const_spec(wd.shape)],
        out_specs=tile(d),
        out_shape=jax.ShapeDtypeStruct((b, s, d), F32),
        compiler_params=_cparams("parallel", "parallel"),
        name="l0_out_ffn",
    )(x, attn, conv, mod, ng, wa, wc, wg, wu, wd)


def _pool_kernel(x_ref, xp_ref, xn_ref, mod_ref, ng1_ref, ng2_ref, pw_ref, ps_ref, rw_ref, rb_ref,
                 x3_ref, f_ref, route_ref, cnt_ref, *, tm, seq):
    i = pl.program_id(1)
    ni = pl.num_programs(1)
    g = ng1_ref[...]
    shift, scale = mod_ref[0:1, :], mod_ref[1:2, :]
    x = x_ref[...]
    a = _rms_mod(x, g, shift, scale)
    a_before = _rms_mod(xp_ref[...], g, shift, scale) * (i > 0).astype(F32)
    a_after = _rms_mod(xn_ref[...], g, shift, scale) * (i < ni - 1).astype(F32)
    ext = jnp.concatenate([a_before, a, a_after], axis=0)

    t = i * tm + lax.broadcasted_iota(jnp.int32, (tm, 1), 0)
    ys = []
    for gi, w in enumerate(POOL_WINDOWS):
        sl = slice(gi * POOL_GROUP_W, (gi + 1) * POOL_GROUP_W)
        e = ext[:, sl]
        acc, span = e, 1
        while span < w:
            acc = acc + pltpu.roll(acc, span, 0)
            span *= 2
        win = acc[POOL_HALO + w // 2 - 1:POOL_HALO + w // 2 - 1 + tm, :]
        cnt = jnp.minimum(t + w // 2, seq) - jnp.maximum(t - w // 2, 0)
        p = (win / cnt.astype(F32) - a[:, sl]).astype(BF16)
        ys.append(jnp.dot(p, pw_ref[gi], preferred_element_type=F32))
    y = jnp.concatenate(ys, axis=1) * ps_ref[...]

    x3 = x + mod_ref[2:3, :] * y
    x3_ref[...] = x3
    f = _rms_mod(x3, ng2_ref[...], mod_ref[3:4, :], mod_ref[4:5, :])
    f_hi = f.astype(BF16)
    f_ref[...] = f_hi
    f_lo = (f - f_hi.astype(F32)).astype(BF16)
    z_hi = jnp.dot(f_hi, rw_ref[...], preferred_element_type=F32)
    logits = (z_hi[:, 0:LANES] + z_hi[:, LANES:2 * LANES]
              + jnp.dot(f_lo, rw_ref[:, 0:LANES], preferred_element_type=F32)) + rb_ref[...]
    lane = lax.broadcasted_iota(jnp.int32, logits.shape, 1)
    neg = jnp.float32(-jnp.inf)
    logits = jnp.where(lane < N_EXPERTS, logits, neg)
    v1 = jnp.max(logits, axis=-1, keepdims=True)
    i1 = jnp.min(jnp.where(logits == v1, lane, LANES), axis=-1, keepdims=True)
    rest = jnp.where(lane == i1, neg, logits)
    v2 = jnp.max(rest, axis=-1, keepdims=True)
    i2 = jnp.min(jnp.where(rest == v2, lane, LANES), axis=-1, keepdims=True)
    e2 = jnp.exp(v2 - v1)
    p1 = 1.0 / (1.0 + e2)
    p2 = e2 / (1.0 + e2)
    i2 = jnp.where(p2 != 0.0, i2, -1)
    route_ref[...] = jnp.where(lane == 0, i1.astype(F32), jnp.where(
        lane == 1, i2.astype(F32), jnp.where(lane == 2, p1, jnp.where(lane == 3, p2, 0.0))))
    routed = jnp.where((lane == i1) | (lane == i2), 1.0, 0.0)
    for h in range(tm // DISPATCH_T):
        cnt = jnp.sum(routed[h * DISPATCH_T:(h + 1) * DISPATCH_T, :], axis=0, keepdims=True)
        cnt_ref[h:h + 1, :] = cnt.astype(jnp.int32)


def _pool_router(x2, mod, ng1, ng2, pw, ps, rw, rb, tm):
    b, s, d = x2.shape
    nt = s // tm
    hb = tm // POOL_HALO
    rwh = rw.astype(BF16)
    rw2 = jnp.concatenate([rwh, (rw - rwh.astype(F32)).astype(BF16)], axis=1)
    tile = lambda w: pl.BlockSpec((None, tm, w), lambda bi, i: (bi, i, 0))
    return pl.pallas_call(
        functools.partial(_pool_kernel, tm=tm, seq=s),
        grid=(b, nt),
        in_specs=[
            tile(d),
            pl.BlockSpec((None, POOL_HALO, d), lambda bi, i: (bi, jnp.maximum(i * hb - 1, 0), 0)),
            pl.BlockSpec((None, POOL_HALO, d),
                         lambda bi, i: (bi, jnp.minimum((i + 1) * hb, nt * hb - 1), 0)),
            pl.BlockSpec((None, SUBLANES, d), lambda bi, i: (bi, 0, 0)),
            _const_spec((1, d)), _const_spec((1, d)), _const_spec(pw.shape), _const_spec((1, d)),
            _const_spec(rw2.shape), _const_spec(rb.shape),
        ],
        out_specs=[tile(d), tile(d), tile(LANES),
                   pl.BlockSpec((None, None, tm // DISPATCH_T, LANES), lambda bi, i: (bi, i, 0, 0))],
        out_shape=[jax.ShapeDtypeStruct((b, s, d), F32), jax.ShapeDtypeStruct((b, s, d), BF16),
                   jax.ShapeDtypeStruct((b, s, LANES), F32),
                   jax.ShapeDtypeStruct((b, nt, tm // DISPATCH_T, LANES), jnp.int32)],
        compiler_params=_cparams("parallel", "parallel"),
        name="pool_router",
    )(x2, x2, x2, mod, ng1, ng2, pw, ps, rw2, rb)


def _seg_copy(vmem_buf, slot, hbm, loc, dst, rows, sem, to_hbm):
    n = pl.multiple_of(rows, SEG_ALIGN)
    v = vmem_buf.at[slot, pl.ds(pl.multiple_of(loc, SEG_ALIGN), n)]
    h = hbm.at[pl.ds(pl.multiple_of(dst, SEG_ALIGN), n)]
    return pltpu.make_async_copy(v, h, sem) if to_hbm else pltpu.make_async_copy(h, v, sem)


def _start_segments(vmem_buf, slot, hbm, loc_s, gdst_s, nch_s, tile, sem, to_hbm):
    for e in range(N_EXPERTS):
        k = tile * N_EXPERTS + e
        rows = nch_s[k] * SEG_ALIGN

        @pl.when(rows > 0)
        def _(k=k, rows=rows):
            _seg_copy(vmem_buf, slot, hbm, loc_s[k], gdst_s[k], rows, sem, to_hbm).start()


def _wait_segments(vmem_buf, slot, hbm, rows, sem, to_hbm):
    @pl.when(rows > 0)
    def _():
        _seg_copy(vmem_buf, slot, hbm, 0, 0, rows, sem, to_hbm).wait()


def _dispatch_kernel(loc_s, gdst_s, nch_s, rows_s, tail_s, f_ref, route_ref, locrow_ref, tril_ref,
                     xs_hbm, d_ref, sbuf, zbuf, sem, zsem, *, g):
    i = pl.program_id(0)
    n_steps = pl.num_programs(0)
    t = DISPATCH_T
    base, other = (i % 2) * g, (1 - i % 2) * g
    row = lax.broadcasted_iota(jnp.int32, (DISPATCH_ROWS, t), 0).astype(F32)
    for sub in range(g):
        rows = slice(sub * t, (sub + 1) * t)
        route = route_ref[rows, :]
        lane = lax.broadcasted_iota(jnp.int32, route.shape, 1).astype(F32)
        first, second = lane == route[:, 0:1], lane == route[:, 1:2]
        onehot = jnp.where(first | second, 1.0, 0.0).astype(BF16)
        earlier = jnp.dot(tril_ref[...], onehot, preferred_element_type=F32)
        row_of = earlier + locrow_ref[sub]
        d_first = jnp.sum(jnp.where(first, row_of, 0.0), axis=1, keepdims=True)
        d_second = jnp.sum(jnp.where(second, row_of, 0.0), axis=1, keepdims=True)
        d_second = jnp.where(route[:, 1:2] < 0.0, -1.0, d_second)
        d = jnp.where(lane == 0.0, d_first, jnp.where(lane == 1.0, d_second, -1.0))
        d_ref[rows, :] = d
        dt = d.T
        perm = jnp.where((row == dt[0:1, :]) | (row == dt[1:2, :]), 1.0, 0.0)
        sbuf[base + sub] = jnp.dot(perm.astype(BF16), f_ref[rows, :], preferred_element_type=F32).astype(BF16)

    def copies(tile, slot, issue):
        if issue:
            _start_segments(sbuf, slot, xs_hbm, loc_s, gdst_s, nch_s, tile, sem.at[slot], True)
        else:
            _wait_segments(sbuf, slot, xs_hbm, rows_s[tile], sem.at[slot], True)

    for sub in range(g):
        copies(i * g + sub, base + sub, True)

    @pl.when(i > 0)
    def _():
        for sub in range(g):
            copies((i - 1) * g + sub, other + sub, False)

    @pl.when(i == n_steps - 1)
    def _():
        zbuf[...] = jnp.zeros(zbuf.shape, BF16)
        n_blk = xs_hbm.shape[0] // EXPERT_RB

        def gap_copy(e, j):
            dst = pl.multiple_of(tail_s[e] + j * SEG_ALIGN, SEG_ALIGN)
            return pltpu.make_async_copy(zbuf.at[pl.ds(0, SEG_ALIGN)], xs_hbm.at[pl.ds(dst, SEG_ALIGN)], zsem)

        def blk_copy(blk):
            dst = pl.multiple_of(blk * EXPERT_RB, EXPERT_RB)
            return pltpu.make_async_copy(zbuf, xs_hbm.at[pl.ds(dst, EXPERT_RB)], zsem)

        def fill(issue):
            def run(copy):
                copy.start() if issue else copy.wait()

            for e in range(N_EXPERTS):
                def gap_body(j, carry, e=e):
                    run(gap_copy(e, j))
                    return carry

                lax.fori_loop(0, tail_s[N_EXPERTS + e], gap_body, 0)

            def blk_body(blk, carry):
                run(blk_copy(blk))
                return carry

            lax.fori_loop(tail_s[2 * N_EXPERTS], n_blk, blk_body, 0)

        fill(True)
        for sub in range(g):
            copies(i * g + sub, base + sub, False)
        fill(False)


def _dispatch(f, route, loc, gdst, nch, rows, tail, n_blk, g):
    n, d = f.shape
    t = DISPATCH_T
    locrow = jnp.pad(loc.astype(F32), ((0, 0), (0, LANES - N_EXPERTS)))[:, None, :]
    tril = (jnp.arange(t)[:, None] > jnp.arange(t)[None, :]).astype(BF16)
    grid_spec = pltpu.PrefetchScalarGridSpec(
        num_scalar_prefetch=5,
        grid=(n // (g * t),),
        in_specs=[pl.BlockSpec((g * t, d), lambda i, *_: (i, 0)),
                  pl.BlockSpec((g * t, LANES), lambda i, *_: (i, 0)),
                  pl.BlockSpec((g, 1, LANES), lambda i, *_: (i, 0, 0)),
                  pl.BlockSpec((t, t), lambda i, *_: (0, 0))],
        out_specs=[pl.BlockSpec(memory_space=pl.ANY),
                   pl.BlockSpec((g * t, LANES), lambda i, *_: (i, 0))],
        scratch_shapes=[pltpu.VMEM((2 * g, DISPATCH_ROWS, d), BF16),
                        pltpu.VMEM((EXPERT_RB, d), BF16),
                        pltpu.SemaphoreType.DMA((2 * g,)),
                        pltpu.SemaphoreType.DMA(())],
    )
    return pl.pallas_call(
        functools.partial(_dispatch_kernel, g=g),
        grid_spec=grid_spec,
        out_shape=[jax.ShapeDtypeStruct((n_blk * EXPERT_RB, d), BF16),
                   jax.ShapeDtypeStruct((n, LANES), F32)],
        compiler_params=_cparams("arbitrary"),
        name="dispatch",
    )(loc.reshape(-1), gdst.reshape(-1), nch.reshape(-1), rows, tail, f, route, locrow, tril)


def _expert_kernel(exp_s, nvalid_s, xs_ref, wg_ref, wu_ref, wd_ref, ys_ref, wgu_ref):
    b = pl.program_id(0)
    used = b < nvalid_s[0]
    dff = wd_ref.shape[0]

    @pl.when((b == 0) | (exp_s[b] != exp_s[jnp.maximum(b - 1, 0)]))
    def _():
        wgu_ref[:, 0:dff] = wg_ref[...]
        wgu_ref[:, dff:2 * dff] = wu_ref[...]

    @pl.when(used)
    def _():
        z = jnp.dot(xs_ref[...], wgu_ref[...], preferred_element_type=F32)
        hg, hu = z[:, 0:dff], z[:, dff:2 * dff]
        h = (hg * (1.0 / (1.0 + jnp.exp(-hg))) * hu).astype(BF16)
        ys_ref[...] = jnp.dot(h, wd_ref[...], preferred_element_type=F32).astype(BF16)

    @pl.when(jnp.logical_not(used))
    def _():
        ys_ref[...] = jnp.zeros(ys_ref.shape, BF16)


def _experts(xs, blk_exp, n_valid, wg, wu, wd):
    rows, d = xs.shape
    dff = wd.shape[1]
    grid_spec = pltpu.PrefetchScalarGridSpec(
        num_scalar_prefetch=2,
        grid=(rows // EXPERT_RB,),
        in_specs=[pl.BlockSpec((EXPERT_RB, d), lambda b, exp_s, nv: (b, 0)),
                  pl.BlockSpec((None, d, dff), lambda b, exp_s, nv: (exp_s[b], 0, 0)),
                  pl.BlockSpec((None, d, dff), lambda b, exp_s, nv: (exp_s[b], 0, 0)),
                  pl.BlockSpec((None, dff, d), lambda b, exp_s, nv: (exp_s[b], 0, 0))],
        out_specs=pl.BlockSpec((EXPERT_RB, d), lambda b, exp_s, nv: (b, 0)),
        scratch_shapes=[pltpu.VMEM((d, 2 * dff), BF16)],
    )
    return pl.pallas_call(
        _expert_kernel,
        grid_spec=grid_spec,
        out_shape=jax.ShapeDtypeStruct((rows, d), BF16),
        compiler_params=_cparams("arbitrary"),
        name="experts",
    )(blk_exp, n_valid, xs, wg, wu, wd)


def _combine_kernel(loc_s, gdst_s, nch_s, rows_s, x_ref, route_ref, d_ref, mod_ref, fg_ref, ys_hbm,
                    o_ref, ybuf, sem, *, g):
    i = pl.program_id(0)
    n_steps = pl.num_programs(0)
    t = DISPATCH_T
    base, other = (i % 2) * g, (1 - i % 2) * g

    def copies(tile, slot, issue):
        if issue:
            _start_segments(ybuf, slot, ys_hbm, loc_s, gdst_s, nch_s, tile, sem.at[slot], False)
        else:
            _wait_segments(ybuf, slot, ys_hbm, rows_s[tile], sem.at[slot], False)

    @pl.when(i == 0)
    def _():
        ybuf[...] = jnp.zeros(ybuf.shape, BF16)
        for sub in range(g):
            copies(sub, base + sub, True)

    @pl.when(i + 1 < n_steps)
    def _():
        for sub in range(g):
            copies((i + 1) * g + sub, other + sub, True)

    for sub in range(g):
        copies(i * g + sub, base + sub, False)
    lane = lax.broadcasted_iota(jnp.int32, (t, DISPATCH_ROWS), 1).astype(F32)
    for sub in range(g):
        rows = slice(sub * t, (sub + 1) * t)
        route, d = route_ref[rows, :], d_ref[rows, :]
        gmat = jnp.where(lane == d[:, 0:1], route[:, 2:3], jnp.where(lane == d[:, 1:2], route[:, 3:4], 0.0))
        moe = jnp.dot(gmat.astype(BF16), ybuf[base + sub], preferred_element_type=F32)
        x4 = x_ref[rows, :] + mod_ref[5:6, :] * moe
        ms = jnp.mean(x4 * x4, axis=-1, keepdims=True)
        o_ref[rows, :] = x4 * lax.rsqrt(ms + EPS) * fg_ref[...]


def _combine(x3, route, dmap, ys, mod, fg, loc, gdst, nch, rows, steps_per_batch, g):
    n, d = x3.shape
    t = DISPATCH_T
    grid_spec = pltpu.PrefetchScalarGridSpec(
        num_scalar_prefetch=4,
        grid=(n // (g * t),),
        in_specs=[pl.BlockSpec((g * t, d), lambda i, *_: (i, 0)),
                  pl.BlockSpec((g * t, LANES), lambda i, *_: (i, 0)),
                  pl.BlockSpec((g * t, LANES), lambda i, *_: (i, 0)),
                  pl.BlockSpec((None, SUBLANES, d), lambda i, *_: (i // steps_per_batch, 0, 0)),
                  pl.BlockSpec((1, d), lambda i, *_: (0, 0)),
                  pl.BlockSpec(memory_space=pl.ANY)],
        out_specs=pl.BlockSpec((g * t, d), lambda i, *_: (i, 0)),
        scratch_shapes=[pltpu.VMEM((2 * g, DISPATCH_ROWS, d), BF16),
                        pltpu.SemaphoreType.DMA((2 * g,))],
    )
    return pl.pallas_call(
        functools.partial(_combine_kernel, g=g),
        grid_spec=grid_spec,
        out_shape=jax.ShapeDtypeStruct((n, d), F32),
        compiler_params=_cparams("arbitrary"),
        name="combine",
    )(loc.reshape(-1), gdst.reshape(-1), nch.reshape(-1), rows, x3, route, dmap, mod, fg, ys)


def _routing_tables(cnt, n_tokens):
    n_tiles = cnt.shape[0]
    i32 = lambda a: a.astype(jnp.int32)
    npad = (cnt + SEG_ALIGN - 1) // SEG_ALIGN * SEG_ALIGN
    loc = jnp.cumsum(npad, axis=1) - npad
    tot = jnp.sum(npad, axis=0)
    nblk = (tot + EXPERT_RB - 1) // EXPERT_RB
    ends = jnp.cumsum(nblk)
    base = (ends - nblk) * EXPERT_RB
    gdst = base[None, :] + jnp.cumsum(npad, axis=0) - npad
    nch = npad // SEG_ALIGN
    n_blk = (2 * n_tokens + (SEG_ALIGN - 1) * N_EXPERTS * n_tiles) // EXPERT_RB + N_EXPERTS
    blk_exp = jnp.minimum(jnp.searchsorted(ends, jnp.arange(n_blk), side="right"), N_EXPERTS - 1)
    tail = jnp.concatenate([base + tot, (nblk * EXPERT_RB - tot) // SEG_ALIGN, ends[-1:]])
    rows = jnp.sum(npad, axis=1)
    return n_blk, i32(loc), i32(gdst), i32(nch), i32(rows), i32(tail), i32(blk_exp), i32(ends[-1:])


def _rope_tables(n):
    t = jnp.arange(n)
    pos = jnp.stack([t // GRID_W, t % GRID_W], axis=1).astype(F32)
    freqs = ROPE_BASE ** (-jnp.arange(0, ROPE_AXIS_DIM, 2, dtype=F32) / ROPE_AXIS_DIM)
    ang = pos[:, :, None] * freqs
    ang = jnp.broadcast_to(ang[:, :, None, :], (n, 2, 2, ROPE_AXIS_DIM // 2))
    sign = jnp.array([-1.0, 1.0], F32)[None, None, :, None]
    cos = jnp.cos(ang).reshape(n, HEAD_DIM)
    sin = (jnp.sin(ang) * sign).reshape(n, HEAD_DIM)
    return jnp.tile(cos, (1, LANES // HEAD_DIM)), jnp.tile(sin, (1, LANES // HEAD_DIM))


def _tile_sizes(s):
    return min(1024, s), min(512, s), min(512, s), min(2048, s)


def _pad_rows(m, rows):
    return jnp.pad(m, ((0, 0),) * (m.ndim - 2) + ((0, rows - m.shape[-2]), (0, 0)))


def kernel(x, c, ctx, c_ctx, w_mod, b_mod, norm_g, final_norm_g, w_mix_in, q_norm_g, k_norm_g, conv_w, w_mix_out, ffn_w_gate, ffn_w_up, ffn_w_down, pool_w, pool_scale, router_w, router_b, exp_w_gate, exp_w_up, exp_w_down):
    b, s, d = x.shape
    tm, tm_ffn, tq, tk = _tile_sizes(s)

    rows = -(-(b + 1) // SUBLANES) * SUBLANES
    cvec = _pad_rows(jnp.concatenate([c, c_ctx[None, :]], axis=0), rows)
    mods = _adaln(cvec, w_mod, b_mod).reshape(w_mod.shape[0], rows, N_MOD, d)
    mod_lat = [_pad_rows(mods[l, :b], SUBLANES) for l in range(2)]
    mod_ctx0 = _pad_rows(mods[0, b], SUBLANES)

    w_in = w_mix_in[0].astype(BF16)
    cuts = (ATTN_W, ATTN_W + KV_W, ATTN_W + 2 * KV_W)
    wq, wkv, wc = w_in[:, :cuts[0]], w_in[:, cuts[0]:cuts[2]], w_in[:, cuts[2]:]
    head = jnp.arange(ATTN_W) // HEAD_DIM
    bd = ((head[:, None] == head[None, :]).astype(F32) / HEAD_DIM).astype(BF16)
    qg = jnp.tile(q_norm_g[0], N_HEADS)[None, :]
    kg = jnp.tile(k_norm_g[0], N_KV_HEADS)[None, :]
    cw = _pad_rows(conv_w[0], SUBLANES)
    cos, sin = _rope_tables(s)
    ng0a, ng0b = norm_g[0, 0][None, :], norm_g[0, 1][None, :]

    shift = (Q_SCALE * HEAD_DIM * SHIFT_MARGIN) * jnp.max(jnp.abs(q_norm_g[0])) * jnp.max(jnp.abs(k_norm_g[0]))
    qbias = jnp.where(jnp.arange(LANES - HEAD_DIM)[:, None] == 0, -shift, 0.0).astype(F32) * jnp.ones((1, tm), F32)
    q, kext, vext, conv = _l0_in(x, mod_lat[0], ng0a, wq, wkv, wc, bd, qg, kg, cw, cos, sin, qbias, tm)
    kext_c, vext_c = _ctx_kv(ctx, mod_ctx0, ng0a, wkv, bd, kg)
    attn = lax.cond(
        shift <= MAX_FIXED_SHIFT,
        lambda *a: _attention(*a, tq, tk, False),
        lambda *a: _attention(*a, tq, tk, True),
        q, kext_c, vext_c, kext, vext)

    w_out = w_mix_out[0].astype(BF16)
    x2 = _l0_out_ffn(x, attn, conv, mod_lat[0], ng0b, w_out[:ATTN_W], w_out[ATTN_W:],
                     ffn_w_gate[0].astype(BF16), ffn_w_up[0].astype(BF16), ffn_w_down[0].astype(BF16), tm_ffn)

    rw = jnp.pad(router_w[0], ((0, 0), (0, LANES - N_EXPERTS)))
    rb = jnp.pad(router_b[0], (0, LANES - N_EXPERTS))[None, :]
    x3, f2, route, cnt = _pool_router(x2, mod_lat[1], norm_g[1, 0][None, :], norm_g[1, 1][None, :],
                                      pool_w[0].astype(BF16), pool_scale[0][None, :], rw, rb, tm)
    n = b * s
    n_blk, loc, gdst, nch, rows, tail, blk_exp, n_valid = _routing_tables(
        cnt.reshape(n // DISPATCH_T, LANES)[:, :N_EXPERTS], n)
    route = route.reshape(n, LANES)
    tiles_per_batch = s // DISPATCH_T
    g_dispatch = max(k for k in (8, 4, 2, 1) if tiles_per_batch % k == 0)
    g = max(k for k in (4, 2, 1) if tiles_per_batch % k == 0)
    xs, dmap = _dispatch(f2.reshape(n, d), route, loc, gdst, nch, rows, tail, n_blk, g_dispatch)
    ys = _experts(xs, blk_exp, n_valid, exp_w_gate[0].astype(BF16), exp_w_up[0].astype(BF16),
                  exp_w_down[0].astype(BF16))
    out = _combine(x3.reshape(n, d), route, dmap, ys, mod_lat[1], final_norm_g[None, :],
                   loc, gdst, nch, rows, tiles_per_batch // g, g)
    return out.reshape(b, s, d)
```
